```python
import math
import jax, jax.numpy as jnp
from jax import lax
import numpy as np

D_MODEL = 1024
BATCH = 4
SEQ = 8192
DEPTH = 1

N_META = 16
D_ATTN = D_MODEL // 2
D_CONV = D_MODEL // 2
D_MIX = D_ATTN + D_CONV
HEAD_DIM = 64
N_DIFF_HEADS = D_ATTN // (2 * HEAD_DIM)
CONV_WIDTH = 31
D_FF = 2816
ROPE_THETA = 10000.0
Q_BLOCK = 128
NORM_EPS = 1e-5
D_IN_PROJ = 3 * D_ATTN + 2 * D_CONV

kernel_name = "hymba_diffattn_conformer_macaron"


def rmsnorm(x, g):
    xf = x.astype(jnp.float32)
    y = xf * lax.rsqrt(jnp.mean(xf * xf, axis=-1, keepdims=True) + NORM_EPS)
    return (y * g.astype(jnp.float32)).astype(x.dtype)


def swiglu_ffn(x, w_gate, w_up, w_down):
    return (jax.nn.silu(x @ w_gate) * (x @ w_up)) @ w_down


def rope_tables(length):
    pos = jnp.arange(length, dtype=jnp.float32)
    inv_freq = ROPE_THETA ** (-jnp.arange(0, HEAD_DIM, 2, dtype=jnp.float32) / HEAD_DIM)
    ang = pos[:, None] * inv_freq[None, :]
    return jnp.cos(ang), jnp.sin(ang)


def apply_rope(x, cos, sin):
    half = HEAD_DIM // 2
    x1, x2 = x[..., :half], x[..., half:]
    return jnp.concatenate([x1 * cos - x2 * sin, x2 * cos + x1 * sin], axis=-1)


def diff_attention(q, k, v, lam, subln_w, lambda_init):
    B, L, _ = q.shape
    H = N_DIFF_HEADS
    n_blk = -(-L // Q_BLOCK)
    Lp = n_blk * Q_BLOCK
    pad = ((0, 0), (0, Lp - L), (0, 0))
    q = jnp.pad(q, pad).reshape(B, Lp, 2 * H, HEAD_DIM).transpose(0, 2, 1, 3).astype(jnp.float32)
    k = jnp.pad(k, pad).reshape(B, Lp, 2 * H, HEAD_DIM).transpose(0, 2, 1, 3).astype(jnp.float32)
    v = jnp.pad(v, pad).reshape(B, Lp, H, 2 * HEAD_DIM).transpose(0, 2, 1, 3).astype(jnp.float32)
    cos, sin = rope_tables(Lp)
    q = apply_rope(q, cos, sin) * (HEAD_DIM ** -0.5)
    k = apply_rope(k, cos, sin)
    kpos = jnp.arange(Lp)

    def block(i):
        start = i * Q_BLOCK
        qb = lax.dynamic_slice_in_dim(q, start, Q_BLOCK, axis=2)
        s = jnp.einsum('bhqd,bhkd->bhqk', qb, k)
        qpos = start + jnp.arange(Q_BLOCK)
        s = jnp.where(kpos[None, :] <= qpos[:, None], s, -jnp.inf)
        p = jax.nn.softmax(s, axis=-1).reshape(B, H, 2, Q_BLOCK, Lp)
        p = p[:, :, 0] - lam * p[:, :, 1]
        return jnp.einsum('bhqk,bhkv->bhqv', p, v)

    o = lax.map(block, jnp.arange(n_blk))
    o = o.transpose(1, 0, 3, 2, 4).reshape(B, Lp, H, 2 * HEAD_DIM)[:, :L]
    o = rmsnorm(o, subln_w) * (1.0 - lambda_init)
    return o.reshape(B, L, D_ATTN)


def conformer_conv(u, conv_w, conv_b, ln_g, ln_b):
    a, g = jnp.split(u, 2, axis=-1)
    z = a * jax.nn.sigmoid(g)
    z = lax.conv_general_dilated(
        z, conv_w[:, None, :].astype(z.dtype), window_strides=(1,),
        padding=[(CONV_WIDTH - 1, 0)],
        dimension_numbers=('NWC', 'WIO', 'NWC'),
        feature_group_count=D_CONV) + conv_b
    zf = z.astype(jnp.float32)
    mu = jnp.mean(zf, axis=-1, keepdims=True)
    var = jnp.mean(jnp.square(zf - mu), axis=-1, keepdims=True)
    zf = (zf - mu) * lax.rsqrt(var + NORM_EPS) * ln_g.astype(jnp.float32) + ln_b.astype(jnp.float32)
    return jax.nn.silu(zf).astype(u.dtype)


def setup_inputs(seed: int = 0) -> dict:
    key = jax.random.key(seed)
    ks = jax.random.split(key, 24)
    f32 = jnp.float32

    def nrm(k, shape, scale):
        return jax.random.normal(k, shape, f32) * scale

    def gain(k, shape):
        return 1.0 + 0.02 * jax.random.normal(k, shape, f32)

    return {
        "x": jax.random.normal(ks[0], (BATCH, SEQ, D_MODEL), f32),
        "meta_tokens": nrm(ks[1], (N_META, D_MODEL), 1.0),
        "ffn1_norm": gain(ks[2], (DEPTH, D_MODEL)),
        "ffn1_w_gate": nrm(ks[3], (DEPTH, D_MODEL, D_FF), D_MODEL ** -0.5),
        "ffn1_w_up": nrm(ks[4], (DEPTH, D_MODEL, D_FF), D_MODEL ** -0.5),
        "ffn1_w_down": nrm(ks[5], (DEPTH, D_FF, D_MODEL), D_FF ** -0.5),
        "mix_norm": gain(ks[6], (DEPTH, D_MODEL)),
        "w_in": nrm(ks[7], (DEPTH, D_MODEL, D_IN_PROJ), D_MODEL ** -0.5),
        "lambda_q1": nrm(ks[8], (DEPTH, HEAD_DIM), 0.1),
        "lambda_k1": nrm(ks[9], (DEPTH, HEAD_DIM), 0.1),
        "lambda_q2": nrm(ks[10], (DEPTH, HEAD_DIM), 0.1),
        "lambda_k2": nrm(ks[11], (DEPTH, HEAD_DIM), 0.1),
        "subln_w": gain(ks[12], (DEPTH, 2 * HEAD_DIM)),
        "conv_w": nrm(ks[13], (DEPTH, CONV_WIDTH, D_CONV), CONV_WIDTH ** -0.5),
        "conv_b": nrm(ks[14], (DEPTH, D_CONV), 0.02),
        "conv_ln_g": gain(ks[15], (DEPTH, D_CONV)),
        "conv_ln_b": nrm(ks[16], (DEPTH, D_CONV), 0.02),
        "w_out": nrm(ks[17], (DEPTH, D_MIX, D_MODEL), D_MIX ** -0.5),
        "ffn2_norm": gain(ks[18], (DEPTH, D_MODEL)),
        "ffn2_w_gate": nrm(ks[19], (DEPTH, D_MODEL, D_FF), D_MODEL ** -0.5),
        "ffn2_w_up": nrm(ks[20], (DEPTH, D_MODEL, D_FF), D_MODEL ** -0.5),
        "ffn2_w_down": nrm(ks[21], (DEPTH, D_FF, D_MODEL), D_FF ** -0.5),
        "final_norm": gain(ks[22], (D_MODEL,)),
    }


def reference(x, meta_tokens, ffn1_norm, ffn1_w_gate, ffn1_w_up, ffn1_w_down,
              mix_norm, w_in, lambda_q1, lambda_k1, lambda_q2, lambda_k2, subln_w,
              conv_w, conv_b, conv_ln_g, conv_ln_b, w_out,
              ffn2_norm, ffn2_w_gate, ffn2_w_up, ffn2_w_down, final_norm):
    B = x.shape[0]
    meta = jnp.broadcast_to(meta_tokens.astype(x.dtype)[None], (B, N_META, D_MODEL))
    h_res = jnp.concatenate([meta, x], axis=1)

    for l in range(DEPTH):
        h = rmsnorm(h_res, ffn1_norm[l])
        h_res = h_res + 0.5 * swiglu_ffn(h, ffn1_w_gate[l], ffn1_w_up[l], ffn1_w_down[l])

        h = rmsnorm(h_res, mix_norm[l])
        proj = h @ w_in[l]
        q, k, v, u = jnp.split(proj, [D_ATTN, 2 * D_ATTN, 3 * D_ATTN], axis=-1)
        lambda_init = 0.8 - 0.6 * math.exp(-0.3 * l)
        lam = (jnp.exp(jnp.sum(lambda_q1[l].astype(jnp.float32) * lambda_k1[l].astype(jnp.float32)))
               - jnp.exp(jnp.sum(lambda_q2[l].astype(jnp.float32) * lambda_k2[l].astype(jnp.float32)))
               + lambda_init)
        a = diff_attention(q, k, v, lam, subln_w[l], lambda_init).astype(h_res.dtype)
        c = conformer_conv(u, conv_w[l], conv_b[l], conv_ln_g[l], conv_ln_b[l])
        h_res = h_res + jnp.concatenate([a, c], axis=-1) @ w_out[l]

        h = rmsnorm(h_res, ffn2_norm[l])
        h_res = h_res + 0.5 * swiglu_ffn(h, ffn2_w_gate[l], ffn2_w_up[l], ffn2_w_down[l])

    y = rmsnorm(h_res, final_norm)
    return y[:, N_META:]
```

```python
import functools
import math

import jax
import jax.numpy as jnp
from jax import lax
from jax.experimental import pallas as pl
from jax.experimental.pallas import tpu as pltpu

D_MODEL = 1024
N_META = 16
D_ATTN = 512
D_CONV = 512
HEAD_DIM = 64
N_DIFF_HEADS = 4
CONV_WIDTH = 31
D_FF = 2816
ROPE_THETA = 10000.0
NORM_EPS = 1e-5
D_IN_PROJ = 3 * D_ATTN + 2 * D_CONV
LAMBDA_INIT = 0.8 - 0.6 * math.exp(-0.3 * 0)

LANES = 128
HALO = 32
ROW_TILE = 512
Q_TILE = 512
KV_TILE = 512
VMEM_LIMIT_BYTES = 56 * 1024 * 1024
MASK_VALUE = -1e30


def _rmsnorm(x, g):
    return x * lax.rsqrt(jnp.mean(x * x, axis=-1, keepdims=True) + NORM_EPS) * g


def _swiglu(h_bf16, wg_ref, wu_ref, wd_ref):
    g = jnp.dot(h_bf16, wg_ref[...], preferred_element_type=jnp.float32)
    u = jnp.dot(h_bf16, wu_ref[...], preferred_element_type=jnp.float32)
    a = (g * jax.nn.sigmoid(g) * u).astype(jnp.bfloat16)
    return jnp.dot(a, wd_ref[...], preferred_element_type=jnp.float32)


def _rope(x, cos, sin_lo, sin_hi):
    return x * cos + pltpu.roll(x, 96, 1) * sin_lo + pltpu.roll(x, 32, 1) * sin_hi


def _ffn1_proj_kernel(x_ref, n1_ref, wg_ref, wu_ref, wd_ref, nm_ref, win_ref, invf_ref,
                      x1_ref, q_ref, k_ref, v_ref, z_ref, *, tiles_per_seq, pos_offset):
    tr = x_ref.shape[0]
    x = x_ref[...]
    h = _rmsnorm(x, n1_ref[...]).astype(jnp.bfloat16)
    x1 = x + 0.5 * _swiglu(h, wg_ref, wu_ref, wd_ref)
    x1_ref[...] = x1

    hm = _rmsnorm(x1, nm_ref[...]).astype(jnp.bfloat16)
    proj = jnp.dot(hm, win_ref[...], preferred_element_type=jnp.float32)

    t0 = (pl.program_id(0) % tiles_per_seq) * tr + pos_offset
    pos = (t0 + lax.broadcasted_iota(jnp.int32, (tr, LANES), 0)).astype(jnp.float32)
    ang = pos * invf_ref[...]
    cos = jnp.cos(ang)
    sin = jnp.sin(ang)
    lane = lax.broadcasted_iota(jnp.int32, (tr, LANES), 1)
    first_half = (lane % HEAD_DIM) < (HEAD_DIM // 2)
    sin_lo = jnp.where(first_half, -sin, 0.0)
    sin_hi = jnp.where(first_half, 0.0, sin)
    scale = HEAD_DIM ** -0.5
    for c in range(D_ATTN // LANES):
        sl = slice(c * LANES, (c + 1) * LANES)
        qc = proj[:, c * LANES:(c + 1) * LANES]
        kc = proj[:, D_ATTN + c * LANES:D_ATTN + (c + 1) * LANES]
        q_ref[:, sl] = (_rope(qc, cos, sin_lo, sin_hi) * scale).astype(jnp.bfloat16)
        k_ref[:, sl] = _rope(kc, cos, sin_lo, sin_hi).astype(jnp.bfloat16)
    v_ref[...] = proj[:, 2 * D_ATTN:3 * D_ATTN].astype(jnp.bfloat16)
    ua = proj[:, 3 * D_ATTN:3 * D_ATTN + D_CONV]
    ug = proj[:, 3 * D_ATTN + D_CONV:]
    z_ref[...] = ua * jax.nn.sigmoid(ug)


def _resident(shape):
    return pl.BlockSpec(shape, lambda i: (0,) * len(shape), pipeline_mode=pl.Buffered(1))


def _ffn1_proj(x2d, n1, wg, wu, wd, nm, win, invf, *, row_tile, rows_per_seq, pos_offset):
    rows = x2d.shape[0]
    assert rows % row_tile == 0 and rows_per_seq % row_tile == 0
    row_spec = lambda w: pl.BlockSpec((row_tile, w), lambda i: (i, 0))
    kern = functools.partial(_ffn1_proj_kernel, tiles_per_seq=rows_per_seq // row_tile,
                             pos_offset=pos_offset)
    return pl.pallas_call(
        kern,
        grid=(rows // row_tile,),
        in_specs=[row_spec(D_MODEL), _resident((1, D_MODEL)),
                  _resident((D_MODEL, D_FF)), _resident((D_MODEL, D_FF)), _resident((D_FF, D_MODEL)),
                  _resident((1, D_MODEL)), _resident((D_MODEL, D_IN_PROJ)), _resident((1, LANES))],
        out_specs=[row_spec(D_MODEL), row_spec(D_ATTN), row_spec(D_ATTN), row_spec(D_ATTN),
                   row_spec(D_CONV)],
        out_shape=[jax.ShapeDtypeStruct((rows, D_MODEL), jnp.float32),
                   jax.ShapeDtypeStruct((rows, D_ATTN), jnp.bfloat16),
                   jax.ShapeDtypeStruct((rows, D_ATTN), jnp.bfloat16),
                   jax.ShapeDtypeStruct((rows, D_ATTN), jnp.bfloat16),
                   jax.ShapeDtypeStruct((rows, D_CONV), jnp.float32)],
        compiler_params=pltpu.CompilerParams(dimension_semantics=("arbitrary",),
                                             vmem_limit_bytes=VMEM_LIMIT_BYTES),
        name="ffn1_proj",
    )(x2d, n1, wg, wu, wd, nm, win, invf)


def _attn_kernel(q_ref, k_ref, v_ref, km_ref, vm_ref, lq1_ref, lk1_ref, lq2_ref, lk2_ref, sw_ref,
                 o_ref, m_scr, l_scr, acc_scr):
    tq = q_ref.shape[1]
    tk = KV_TILE
    qi = pl.program_id(2)
    nt = (((1,), (1,)), ((), ()))

    q = q_ref[0]
    lane = lax.broadcasted_iota(jnp.int32, q.shape, 1)
    zero = jnp.zeros_like(q)
    qs = jnp.concatenate([jnp.where(lane < HEAD_DIM, q, zero), jnp.where(lane >= HEAD_DIM, q, zero)], axis=0)

    s = lax.dot_general(qs, km_ref[...], nt, preferred_element_type=jnp.float32)
    col = lax.broadcasted_iota(jnp.int32, s.shape, 1)
    s = jnp.where(col < N_META, s, MASK_VALUE)
    m = jnp.max(s, axis=-1, keepdims=True)
    p = jnp.exp(s - m)
    m_scr[...] = m
    l_scr[...] = jnp.sum(p, axis=-1, keepdims=True)
    acc_scr[...] = jnp.dot(p.astype(jnp.bfloat16), vm_ref[...], preferred_element_type=jnp.float32)

    def step(j, masked):
        start = pl.multiple_of(j * tk, tk)
        kb = k_ref[0, pl.ds(start, tk), :]
        vb = v_ref[0, pl.ds(start, tk), :]
        s = lax.dot_general(qs, kb, nt, preferred_element_type=jnp.float32)
        if masked:
            row = lax.broadcasted_iota(jnp.int32, s.shape, 0) % tq
            col = lax.broadcasted_iota(jnp.int32, s.shape, 1)
            s = jnp.where(col <= row, s, MASK_VALUE)
        m_prev = m_scr[...]
        m_new = jnp.maximum(m_prev, jnp.max(s, axis=-1, keepdims=True))
        alpha = jnp.exp(m_prev - m_new)
        p = jnp.exp(s - m_new)
        l_scr[...] = alpha * l_scr[...] + jnp.sum(p, axis=-1, keepdims=True)
        acc_scr[...] = alpha * acc_scr[...] + jnp.dot(p.astype(jnp.bfloat16), vb,
                                                      preferred_element_type=jnp.float32)
        m_scr[...] = m_new

    def body(j, carry):
        step(j, masked=False)
        return carry

    lax.fori_loop(0, qi, body, 0)
    step(qi, masked=True)

    lam = (jnp.exp(jnp.sum(lq1_ref[...] * lk1_ref[...], axis=-1, keepdims=True))
           - jnp.exp(jnp.sum(lq2_ref[...] * lk2_ref[...], axis=-1, keepdims=True)) + LAMBDA_INIT)
    o = acc_scr[...] / l_scr[...]
    o = o[:tq] - lam * o[tq:]
    o_ref[0] = (_rmsnorm(o, sw_ref[...]) * (1.0 - LAMBDA_INIT)).astype(o_ref.dtype)


def _diff_attention(q, k, v, km, vm, lq1, lk1, lq2, lk2, sw):
    b, t, _ = q.shape
    assert Q_TILE == KV_TILE and t % Q_TILE == 0
    hw = 2 * HEAD_DIM
    small = lambda shape: pl.BlockSpec(shape, lambda bi, hi, qi: (0,) * len(shape))
    return pl.pallas_call(
        _attn_kernel,
        grid=(b, N_DIFF_HEADS, t // Q_TILE),
        in_specs=[pl.BlockSpec((1, Q_TILE, hw), lambda bi, hi, qi: (bi, qi, hi)),
                  pl.BlockSpec((1, t, hw), lambda bi, hi, qi: (bi, 0, hi)),
                  pl.BlockSpec((1, t, hw), lambda bi, hi, qi: (bi, 0, hi)),
                  pl.BlockSpec((LANES, hw), lambda bi, hi, qi: (0, hi)),
                  pl.BlockSpec((LANES, hw), lambda bi, hi, qi: (0, hi)),
                  small((1, HEAD_DIM)), small((1, HEAD_DIM)), small((1, HEAD_DIM)), small((1, HEAD_DIM)),
                  small((1, hw))],
        out_specs=pl.BlockSpec((1, Q_TILE, hw), lambda bi, hi, qi: (bi, qi, hi)),
        out_shape=jax.ShapeDtypeStruct((b, t, D_ATTN), jnp.bfloat16),
        scratch_shapes=[pltpu.VMEM((2 * Q_TILE, 1), jnp.float32),
                        pltpu.VMEM((2 * Q_TILE, 1), jnp.float32),
                        pltpu.VMEM((2 * Q_TILE, hw), jnp.float32)],
        compiler_params=pltpu.CompilerParams(dimension_semantics=("arbitrary", "arbitrary", "arbitrary"),
                                             vmem_limit_bytes=VMEM_LIMIT_BYTES),
        name="diff_attn",
    )(q, k, v, km, vm, lq1, lk1, lq2, lk2, sw)


def _mix_ffn2_kernel(x1_ref, a_ref, z_ref, zprev_ref, zstart_ref, cw_ref, cb_ref, lg_ref, lb_ref,
                     wo_ref, n2_ref, wg_ref, wu_ref, wd_ref, nf_ref, y_ref, zw_scr, *, tiles_per_seq):
    tr = x1_ref.shape[0]
    first = (pl.program_id(0) % tiles_per_seq) == 0

    @pl.when(first)
    def _():
        zw_scr[0:HALO, :] = zstart_ref[...]

    @pl.when(jnp.logical_not(first))
    def _():
        zw_scr[0:HALO, :] = zprev_ref[...]

    zw_scr[HALO:, :] = z_ref[...]
    base = HALO - (CONV_WIDTH - 1)
    conv = jnp.zeros((tr, D_CONV), jnp.float32) + cb_ref[...]
    for j in range(CONV_WIDTH):
        conv = conv + cw_ref[j:j + 1, :] * zw_scr[base + j:base + j + tr, :]
    mu = jnp.mean(conv, axis=-1, keepdims=True)
    cc = conv - mu
    var = jnp.mean(cc * cc, axis=-1, keepdims=True)
    c = cc * lax.rsqrt(var + NORM_EPS) * lg_ref[...] + lb_ref[...]
    c = (c * jax.nn.sigmoid(c)).astype(jnp.bfloat16)

    x2 = (x1_ref[...]
          + jnp.dot(a_ref[...], wo_ref[0:D_ATTN, :], preferred_element_type=jnp.float32)
          + jnp.dot(c, wo_ref[D_ATTN:, :], preferred_element_type=jnp.float32))
    h = _rmsnorm(x2, n2_ref[...]).astype(jnp.bfloat16)
    x3 = x2 + 0.5 * _swiglu(h, wg_ref, wu_ref, wd_ref)
    y_ref[...] = _rmsnorm(x3, nf_ref[...])


def _mix_ffn2(x1, a, z, zstart, cw, cb, lg, lb, wo, n2, wg, wu, wd, nf, *, rows_per_seq):
    rows = x1.shape[0]
    tr = ROW_TILE
    assert rows % tr == 0 and rows_per_seq % tr == 0 and tr % HALO == 0
    row_spec = lambda w: pl.BlockSpec((tr, w), lambda i: (i, 0))
    halo_spec = pl.BlockSpec((HALO, D_CONV), lambda i: (jnp.maximum(i * (tr // HALO) - 1, 0), 0))
    kern = functools.partial(_mix_ffn2_kernel, tiles_per_seq=rows_per_seq // tr)
    return pl.pallas_call(
        kern,
        grid=(rows // tr,),
        in_specs=[row_spec(D_MODEL), row_spec(D_ATTN), row_spec(D_CONV), halo_spec,
                  _resident((HALO, D_CONV)), _resident((CONV_WIDTH, D_CONV)), _resident((1, D_CONV)),
                  _resident((1, D_CONV)), _resident((1, D_CONV)),
                  _resident((D_ATTN + D_CONV, D_MODEL)), _resident((1, D_MODEL)),
                  _resident((D_MODEL, D_FF)), _resident((D_MODEL, D_FF)), _resident((D_FF, D_MODEL)),
                  _resident((1, D_MODEL))],
        out_specs=row_spec(D_MODEL),
        out_shape=jax.ShapeDtypeStruct((rows, D_MODEL), jnp.float32),
        scratch_shapes=[pltpu.VMEM((HALO + tr, D_CONV), jnp.float32)],
        compiler_params=pltpu.CompilerParams(dimension_semantics=("arbitrary",),
                                             vmem_limit_bytes=VMEM_LIMIT_BYTES),
        name="mix_ffn2",
    )(x1, a, z, z, zstart, cw, cb, lg, lb, wo, n2, wg, wu, wd, nf)


def kernel(x, meta_tokens, ffn1_norm, ffn1_w_gate, ffn1_w_up, ffn1_w_down, mix_norm, w_in, lambda_q1, lambda_k1, lambda_q2, lambda_k2, subln_w, conv_w, conv_b, conv_ln_g, conv_ln_b, w_out, ffn2_norm, ffn2_w_gate, ffn2_w_up, ffn2_w_down, final_norm):
    b, t, d = x.shape
    bf16 = jnp.bfloat16
    row = lambda v: v.reshape(1, -1)

    inv_freq = ROPE_THETA ** (-jnp.arange(0, HEAD_DIM, 2, dtype=jnp.float32) / HEAD_DIM)
    invf = jnp.tile(inv_freq, LANES // (HEAD_DIM // 2)).reshape(1, LANES)

    ffn1_args = (row(ffn1_norm[0]), ffn1_w_gate[0].astype(bf16), ffn1_w_up[0].astype(bf16),
                 ffn1_w_down[0].astype(bf16), row(mix_norm[0]), w_in[0].astype(bf16), invf)
    x1, q, k, v, z = _ffn1_proj(x.reshape(b * t, d), *ffn1_args,
                                row_tile=ROW_TILE, rows_per_seq=t, pos_offset=N_META)
    _, _, km, vm, zm = _ffn1_proj(meta_tokens, *ffn1_args,
                                  row_tile=N_META, rows_per_seq=N_META, pos_offset=0)

    pad_rows = lambda m_, n: jnp.concatenate([m_, jnp.zeros((n - m_.shape[0], m_.shape[1]), m_.dtype)], axis=0)
    a = _diff_attention(q.reshape(b, t, D_ATTN), k.reshape(b, t, D_ATTN), v.reshape(b, t, D_ATTN),
                        pad_rows(km, LANES), pad_rows(vm, LANES),
                        row(lambda_q1[0]), row(lambda_k1[0]), row(lambda_q2[0]), row(lambda_k2[0]),
                        row(subln_w[0]))

    zstart = jnp.concatenate([jnp.zeros((HALO - N_META, D_CONV), jnp.float32), zm], axis=0)
    y = _mix_ffn2(x1, a.reshape(b * t, D_ATTN), z, zstart, conv_w[0], row(conv_b[0]),
                  row(conv_ln_g[0]), row(conv_ln_b[0]), w_out[0].astype(bf16), row(ffn2_norm[0]),
                  ffn2_w_gate[0].astype(bf16), ffn2_w_up[0].astype(bf16), ffn2_w_down[0].astype(bf16),
                  row(final_norm), rows_per_seq=t)
    return y.reshape(b, t, d)
```

```python
import functools
import math

import jax
import jax.numpy as jnp
from jax import lax
from jax.experimental import pallas as pl
from jax.experimental.pallas import tpu as pltpu

D_MODEL = 1024
N_META = 16
D_ATTN = 512
D_CONV = 512
HEAD_DIM = 64
N_DIFF_HEADS = 4
CONV_WIDTH = 31
D_FF = 2816
ROPE_THETA = 10000.0
NORM_EPS = 1e-5
D_IN_PROJ = 3 * D_ATTN + 2 * D_CONV
LAMBDA_INIT = 0.8 - 0.6 * math.exp(-0.3 * 0)

LANES = 128
HALO = 32
ROW_TILE = 512
META_TILE = 128
Q_CHUNK = 256
ONES_ROWS = 16
VMEM_LIMIT_BYTES = 56 * 1024 * 1024
MASK_VALUE = -1e30


def _rmsnorm(x, g):
    return x * lax.rsqrt(jnp.mean(x * x, axis=-1, keepdims=True) + NORM_EPS) * g


def _swiglu(h_bf16, wg_ref, wu_ref, wd_ref):
    g = jnp.dot(h_bf16, wg_ref[...], preferred_element_type=jnp.float32)
    u = jnp.dot(h_bf16, wu_ref[...], preferred_element_type=jnp.float32)
    a = (g * jax.nn.sigmoid(g) * u).astype(jnp.bfloat16)
    return jnp.dot(a, wd_ref[...], preferred_element_type=jnp.float32)


def _rope(x, cos, sin_lo, sin_hi):
    return x * cos + pltpu.roll(x, 96, 1) * sin_lo + pltpu.roll(x, 32, 1) * sin_hi


def _ffn1_proj_kernel(x_ref, n1_ref, wg_ref, wu_ref, wd_ref, nm_ref, win_ref, invf_ref,
                      x1_ref, q_ref, k_ref, vt_ref, z_ref, *, tiles_per_seq, pos_offset):
    tr = x_ref.shape[0]
    x = x_ref[...]
    h = _rmsnorm(x, n1_ref[...]).astype(jnp.bfloat16)
    x1 = x + 0.5 * _swiglu(h, wg_ref, wu_ref, wd_ref)
    x1_ref[...] = x1

    hm = _rmsnorm(x1, nm_ref[...]).astype(jnp.bfloat16)
    proj = jnp.dot(hm, win_ref[...], preferred_element_type=jnp.float32)

    t0 = (pl.program_id(0) % tiles_per_seq) * tr + pos_offset
    pos = (t0 + lax.broadcasted_iota(jnp.int32, (tr, LANES), 0)).astype(jnp.float32)
    ang = pos * invf_ref[...]
    cos = jnp.cos(ang)
    sin = jnp.sin(ang)
    lane = lax.broadcasted_iota(jnp.int32, (tr, LANES), 1)
    first_half = (lane % HEAD_DIM) < (HEAD_DIM // 2)
    sin_lo = jnp.where(first_half, -sin, 0.0)
    sin_hi = jnp.where(first_half, 0.0, sin)
    scale = HEAD_DIM ** -0.5 * math.log2(math.e)
    for c in range(D_ATTN // LANES):
        sl = slice(c * LANES, (c + 1) * LANES)
        qc = proj[:, c * LANES:(c + 1) * LANES]
        kc = proj[:, D_ATTN + c * LANES:D_ATTN + (c + 1) * LANES]
        q_ref[:, sl] = (_rope(qc, cos, sin_lo, sin_hi) * scale).astype(jnp.bfloat16)
        k_ref[:, sl] = _rope(kc, cos, sin_lo, sin_hi).astype(jnp.bfloat16)
    hw = 2 * HEAD_DIM
    for hd in range(N_DIFF_HEADS):
        vh = proj[:, 2 * D_ATTN + hd * hw:2 * D_ATTN + (hd + 1) * hw]
        vt_ref[0, hd, 0] = vh.T.astype(jnp.bfloat16)
    ua = proj[:, 3 * D_ATTN:3 * D_ATTN + D_CONV]
    ug = proj[:, 3 * D_ATTN + D_CONV:]
    z_ref[...] = ua * jax.nn.sigmoid(ug)


def _resident(shape):
    return pl.BlockSpec(shape, lambda i: (0,) * len(shape), pipeline_mode=pl.Buffered(1))


def _ffn1_proj(x2d, n1, wg, wu, wd, nm, win, invf, *, row_tile, rows_per_seq, pos_offset):
    rows = x2d.shape[0]
    assert rows % row_tile == 0 and rows_per_seq % row_tile == 0
    tiles_per_seq = rows_per_seq // row_tile
    hw = 2 * HEAD_DIM
    row_spec = lambda w: pl.BlockSpec((row_tile, w), lambda i: (i, 0))
    vt_spec = pl.BlockSpec((1, N_DIFF_HEADS, 1, hw, row_tile),
                           lambda i: (i // tiles_per_seq, 0, i % tiles_per_seq, 0, 0))
    kern = functools.partial(_ffn1_proj_kernel, tiles_per_seq=tiles_per_seq, pos_offset=pos_offset)
    return pl.pallas_call(
        kern,
        grid=(rows // row_tile,),
        in_specs=[row_spec(D_MODEL), _resident((1, D_MODEL)),
                  _resident((D_MODEL, D_FF)), _resident((D_MODEL, D_FF)), _resident((D_FF, D_MODEL)),
                  _resident((1, D_MODEL)), _resident((D_MODEL, D_IN_PROJ)), _resident((1, LANES))],
        out_specs=[row_spec(D_MODEL), row_spec(D_ATTN), row_spec(D_ATTN), vt_spec, row_spec(D_CONV)],
        out_shape=[jax.ShapeDtypeStruct((rows, D_MODEL), jnp.float32),
                   jax.ShapeDtypeStruct((rows, D_ATTN), jnp.bfloat16),
                   jax.ShapeDtypeStruct((rows, D_ATTN), jnp.bfloat16),
                   jax.ShapeDtypeStruct((rows // rows_per_seq, N_DIFF_HEADS, tiles_per_seq, hw, row_tile),
                                        jnp.bfloat16),
                   jax.ShapeDtypeStruct((rows, D_CONV), jnp.float32)],
        compiler_params=pltpu.CompilerParams(dimension_semantics=("arbitrary",),
                                             vmem_limit_bytes=VMEM_LIMIT_BYTES),
        name="ffn1_proj",
    )(x2d, n1, wg, wu, wd, nm, win, invf)


def _attn_kernel(q_ref, k_ref, vt_ref, km_ref, vmt_ref, lq1_ref, lk1_ref, lq2_ref, lk2_ref, sw_ref,
                 o_ref, m_scr, acc_scr, s0_scr, s1_scr):
    tq = q_ref.shape[1]
    hw, tk = vt_ref.shape[-2:]
    qi = pl.program_id(2)
    nt = (((1,), (1,)), ((), ()))
    n_chunks = 2 * tq // Q_CHUNK

    q = q_ref[0]
    lane = lax.broadcasted_iota(jnp.int32, q.shape, 1)
    zero = jnp.zeros_like(q)
    qs = jnp.concatenate([jnp.where(lane < HEAD_DIM, q, zero), jnp.where(lane >= HEAD_DIM, q, zero)], axis=0)
    q_chunks = [qs[c * Q_CHUNK:(c + 1) * Q_CHUNK] for c in range(n_chunks)]

    def with_ones(vt):
        return jnp.concatenate([vt, jnp.ones((ONES_ROWS, vt.shape[1]), vt.dtype)], axis=0)

    km = km_ref[0:N_META, :]
    pad = jnp.zeros((vmt_ref.shape[-1] - N_META, Q_CHUNK), jnp.bfloat16)
    vmt = with_ones(vmt_ref[0, 0, 0])
    for c in range(n_chunks):
        sl = slice(c * Q_CHUNK, (c + 1) * Q_CHUNK)
        s = lax.dot_general(km, q_chunks[c], nt, preferred_element_type=jnp.float32)
        m = jnp.max(s, axis=0, keepdims=True)
        p_pad = jnp.concatenate([jnp.exp2(s - m).astype(jnp.bfloat16), pad], axis=0)
        m_scr[:, sl] = m
        acc_scr[:, sl] = jnp.dot(vmt, p_pad, preferred_element_type=jnp.float32)

    def scores(j, s_scr):
        kb = k_ref[0, pl.ds(pl.multiple_of(j * tk, tk), tk), :]
        s_scr[...] = lax.dot_general(kb, qs, nt, preferred_element_type=jnp.float32)

    def softmax_pv(j, s_scr, masked):
        vtb = with_ones(vt_ref[0, 0, j])
        for c in range(n_chunks):
            sl = slice(c * Q_CHUNK, (c + 1) * Q_CHUNK)
            s = s_scr[:, sl]
            if masked:
                key = lax.broadcasted_iota(jnp.int32, s.shape, 0)
                qry = lax.broadcasted_iota(jnp.int32, s.shape, 1) + (c * Q_CHUNK) % tq
                s = jnp.where(key <= qry, s, MASK_VALUE)
            m_prev = m_scr[:, sl]
            m_new = jnp.maximum(m_prev, jnp.max(s, axis=0, keepdims=True))
            alpha = jnp.exp2(m_prev - m_new)
            p = jnp.exp2(s - m_new).astype(jnp.bfloat16)
            acc_scr[:, sl] = alpha * acc_scr[:, sl] + jnp.dot(vtb, p, preferred_element_type=jnp.float32)
            m_scr[:, sl] = m_new

    scores(0, s0_scr)

    def pair(p, carry):
        j = 2 * p
        scores(j + 1, s1_scr)
        softmax_pv(j, s0_scr, masked=False)
        scores(j + 2, s0_scr)
        softmax_pv(j + 1, s1_scr, masked=False)
        return carry

    lax.fori_loop(0, qi // 2, pair, 0)

    @pl.when(qi % 2 == 1)
    def _():
        scores(qi, s1_scr)
        softmax_pv(qi - 1, s0_scr, masked=False)
        softmax_pv(qi, s1_scr, masked=True)

    @pl.when(qi % 2 == 0)
    def _():
        softmax_pv(qi, s0_scr, masked=True)

    lam = (jnp.exp(jnp.sum(lq1_ref[...] * lk1_ref[...], axis=-1, keepdims=True))
           - jnp.exp(jnp.sum(lq2_ref[...] * lk2_ref[...], axis=-1, keepdims=True)) + LAMBDA_INIT)
    o = acc_scr[0:hw, :] / acc_scr[hw:hw + 1, :]
    o = o[:, :tq] - lam * o[:, tq:]
    o = o * lax.rsqrt(jnp.mean(o * o, axis=0, keepdims=True) + NORM_EPS)
    o_ref[0] = (o.T * sw_ref[...] * (1.0 - LAMBDA_INIT)).astype(o_ref.dtype)


def _diff_attention(q, k, vt, km, vmt, lq1, lk1, lq2, lk2, sw):
    b, t, _ = q.shape
    n_kv, hw, tk = vt.shape[2:]
    tq = tk
    assert t == n_kv * tk and (2 * tq) % Q_CHUNK == 0 and tq % Q_CHUNK == 0
    small = lambda shape: pl.BlockSpec(shape, lambda bi, hi, qi: (0,) * len(shape))
    return pl.pallas_call(
        _attn_kernel,
        grid=(b, N_DIFF_HEADS, t // tq),
        in_specs=[pl.BlockSpec((1, tq, hw), lambda bi, hi, qi: (bi, qi, hi)),
                  pl.BlockSpec((1, t, hw), lambda bi, hi, qi: (bi, 0, hi)),
                  pl.BlockSpec((1, 1, n_kv, hw, tk), lambda bi, hi, qi: (bi, hi, 0, 0, 0)),
                  pl.BlockSpec((km.shape[0], hw), lambda bi, hi, qi: (0, hi)),
                  pl.BlockSpec((1, 1, 1) + vmt.shape[3:], lambda bi, hi, qi: (0, hi, 0, 0, 0)),
                  small((1, HEAD_DIM)), small((1, HEAD_DIM)), small((1, HEAD_DIM)), small((1, HEAD_DIM)),
                  small((1, hw))],
        out_specs=pl.BlockSpec((1, tq, hw), lambda bi, hi, qi: (bi, qi, hi)),
        out_shape=jax.ShapeDtypeStruct((b, t, D_ATTN), jnp.bfloat16),
        scratch_shapes=[pltpu.VMEM((1, 2 * tq), jnp.float32),
                        pltpu.VMEM((hw + ONES_ROWS, 2 * tq), jnp.float32),
                        pltpu.VMEM((tk, 2 * tq), jnp.float32),
                        pltpu.VMEM((tk, 2 * tq), jnp.float32)],
        compiler_params=pltpu.CompilerParams(dimension_semantics=("arbitrary", "arbitrary", "arbitrary"),
                                             vmem_limit_bytes=VMEM_LIMIT_BYTES),
        name="diff_attn",
    )(q, k, vt, km, vmt, lq1, lk1, lq2, lk2, sw)


def _mix_ffn2_kernel(x1_ref, a_ref, z_ref, zprev_ref, zstart_ref, cw_ref, cb_ref, lg_ref, lb_ref,
                     wo_ref, n2_ref, wg_ref, wu_ref, wd_ref, nf_ref, y_ref, zw_scr, *, tiles_per_seq):
    tr = x1_ref.shape[0]
    first = (pl.program_id(0) % tiles_per_seq) == 0

    @pl.when(first)
    def _():
        zw_scr[0:HALO, :] = zstart_ref[...]

    @pl.when(jnp.logical_not(first))
    def _():
        zw_scr[0:HALO, :] = zprev_ref[...]

    zw_scr[HALO:, :] = z_ref[...]
    base = HALO - (CONV_WIDTH - 1)
    conv = jnp.zeros((tr, D_CONV), jnp.float32) + cb_ref[...]
    for j in range(CONV_WIDTH):
        conv = conv + cw_ref[j:j + 1, :] * zw_scr[base + j:base + j + tr, :]
    mu = jnp.mean(conv, axis=-1, keepdims=True)
    cc = conv - mu
    var = jnp.mean(cc * cc, axis=-1, keepdims=True)
    c = cc * lax.rsqrt(var + NORM_EPS) * lg_ref[...] + lb_ref[...]
    c = (c * jax.nn.sigmoid(c)).astype(jnp.bfloat16)

    x2 = (x1_ref[...]
          + jnp.dot(a_ref[...], wo_ref[0:D_ATTN, :], preferred_element_type=jnp.float32)
          + jnp.dot(c, wo_ref[D_ATTN:, :], preferred_element_type=jnp.float32))
    h = _rmsnorm(x2, n2_ref[...]).astype(jnp.bfloat16)
    x3 = x2 + 0.5 * _swiglu(h, wg_ref, wu_ref, wd_ref)
    y_ref[...] = _rmsnorm(x3, nf_ref[...])


def _mix_ffn2(x1, a, z, zstart, cw, cb, lg, lb, wo, n2, wg, wu, wd, nf, *, rows_per_seq):
    rows = x1.shape[0]
    tr = ROW_TILE
    assert rows % tr == 0 and rows_per_seq % tr == 0 and tr % HALO == 0
    row_spec = lambda w: pl.BlockSpec((tr, w), lambda i: (i, 0))
    halo_spec = pl.BlockSpec((HALO, D_CONV), lambda i: (jnp.maximum(i * (tr // HALO) - 1, 0), 0))
    kern = functools.partial(_mix_ffn2_kernel, tiles_per_seq=rows_per_seq // tr)
    return pl.pallas_call(
        kern,
        grid=(rows // tr,),
        in_specs=[row_spec(D_MODEL), row_spec(D_ATTN), row_spec(D_CONV), halo_spec,
                  _resident((HALO, D_CONV)), _resident((CONV_WIDTH, D_CONV)), _resident((1, D_CONV)),
                  _resident((1, D_CONV)), _resident((1, D_CONV)),
                  _resident((D_ATTN + D_CONV, D_MODEL)), _resident((1, D_MODEL)),
                  _resident((D_MODEL, D_FF)), _resident((D_MODEL, D_FF)), _resident((D_FF, D_MODEL)),
                  _resident((1, D_MODEL))],
        out_specs=row_spec(D_MODEL),
        out_shape=jax.ShapeDtypeStruct((rows, D_MODEL), jnp.float32),
        scratch_shapes=[pltpu.VMEM((HALO + tr, D_CONV), jnp.float32)],
        compiler_params=pltpu.CompilerParams(dimension_semantics=("arbitrary",),
                                             vmem_limit_bytes=VMEM_LIMIT_BYTES),
        name="mix_ffn2",
    )(x1, a, z, z, zstart, cw, cb, lg, lb, wo, n2, wg, wu, wd, nf)


def kernel(x, meta_tokens, ffn1_norm, ffn1_w_gate, ffn1_w_up, ffn1_w_down, mix_norm, w_in, lambda_q1, lambda_k1, lambda_q2, lambda_k2, subln_w, conv_w, conv_b, conv_ln_g, conv_ln_b, w_out, ffn2_norm, ffn2_w_gate, ffn2_w_up, ffn2_w_down, final_norm):
    b, t, d = x.shape
    bf16 = jnp.bfloat16
    row = lambda v: v.reshape(1, -1)

    inv_freq = ROPE_THETA ** (-jnp.arange(0, HEAD_DIM, 2, dtype=jnp.float32) / HEAD_DIM)
    invf = jnp.tile(inv_freq, LANES // (HEAD_DIM // 2)).reshape(1, LANES)

    ffn1_args = (row(ffn1_norm[0]), ffn1_w_gate[0].astype(bf16), ffn1_w_up[0].astype(bf16),
                 ffn1_w_down[0].astype(bf16), row(mix_norm[0]), w_in[0].astype(bf16), invf)
    x1, q, k, vt, z = _ffn1_proj(x.reshape(b * t, d), *ffn1_args,
                                 row_tile=ROW_TILE, rows_per_seq=t, pos_offset=N_META)
    meta = jnp.concatenate([meta_tokens, jnp.zeros((META_TILE - N_META, d), meta_tokens.dtype)], axis=0)
    _, _, km, vmt, zm = _ffn1_proj(meta, *ffn1_args, row_tile=META_TILE, rows_per_seq=META_TILE, pos_offset=0)

    a = _diff_attention(q.reshape(b, t, D_ATTN), k.reshape(b, t, D_ATTN), vt, km, vmt,
                        row(lambda_q1[0]), row(lambda_k1[0]), row(lambda_q2[0]), row(lambda_k2[0]),
                        row(subln_w[0]))

    zstart = jnp.concatenate([jnp.zeros((HALO - N_META, D_CONV), jnp.float32), zm[:N_META]], axis=0)
    y = _mix_ffn2(x1, a.reshape(b * t, D_ATTN), z, zstart, conv_w[0], row(conv_b[0]),
                  row(conv_ln_g[0]), row(conv_ln_b[0]), w_out[0].astype(bf16), row(ffn2_norm[0]),
                  ffn2_w_gate[0].astype(bf16), ffn2_w_up[0].astype(bf16), ffn2_w_down[0].astype(bf16),
                  row(final_norm), rows_per_seq=t)
    return y.reshape(b, t, d)
```

```python
import functools
import math

import jax
import jax.numpy as jnp
from jax import lax
from jax.experimental import pallas as pl
from jax.experimental.pallas import tpu as pltpu

D_MODEL = 1024
N_META = 16
D_ATTN = 512
D_CONV = 512
HEAD_DIM = 64
N_DIFF_HEADS = 4
CONV_WIDTH = 31
D_FF = 2816
ROPE_THETA = 10000.0
NORM_EPS = 1e-5
D_IN_PROJ = 3 * D_ATTN + 2 * D_CONV
LAMBDA_INIT = 0.8 - 0.6 * math.exp(-0.3 * 0)

LANES = 128
SUBLANES = 8
HALO = 32
ROW_TILE = 512
META_TILE = 128
Q_CHUNK = 256
FF_CHUNK = 256
CONV_ROWS = 64
SCORE_PAD = 128
TOKEN_LAG = 2
ONES_ROWS = 16
VMEM_LIMIT_BYTES = 56 * 1024 * 1024
MASK_VALUE = -1e30


def _rmsnorm(x, g):
    return x * lax.rsqrt(jnp.mean(x * x, axis=-1, keepdims=True) + NORM_EPS) * g


def _swiglu(h_bf16, wg_ref, wu_ref, wd_ref):
    g = jnp.dot(h_bf16, wg_ref[...], preferred_element_type=jnp.float32)
    u = jnp.dot(h_bf16, wu_ref[...], preferred_element_type=jnp.float32)
    a = (g * jax.nn.sigmoid(g) * u).astype(jnp.bfloat16)
    return jnp.dot(a, wd_ref[...], preferred_element_type=jnp.float32)


def _rope(x, cos, sin_lo, sin_hi):
    return x * cos + pltpu.roll(x, 96, 1) * sin_lo + pltpu.roll(x, 32, 1) * sin_hi


def _ffn1_proj_kernel(x_ref, n1_ref, wg_ref, wu_ref, wd_ref, nm_ref, win_ref, invf_ref,
                      x1_ref, q_ref, k_ref, vt_ref, z_ref, *, tiles_per_seq, pos_offset):
    tr = x_ref.shape[0]
    x = x_ref[...]
    h = _rmsnorm(x, n1_ref[...]).astype(jnp.bfloat16)
    x1 = x + 0.5 * _swiglu(h, wg_ref, wu_ref, wd_ref)
    x1_ref[...] = x1

    hm = _rmsnorm(x1, nm_ref[...]).astype(jnp.bfloat16)
    proj = jnp.dot(hm, win_ref[...], preferred_element_type=jnp.float32)

    t0 = (pl.program_id(0) % tiles_per_seq) * tr + pos_offset
    pos = (t0 + lax.broadcasted_iota(jnp.int32, (tr, LANES), 0)).astype(jnp.float32)
    ang = pos * invf_ref[...]
    cos = jnp.cos(ang)
    sin = jnp.sin(ang)
    lane = lax.broadcasted_iota(jnp.int32, (tr, LANES), 1)
    first_half = (lane % HEAD_DIM) < (HEAD_DIM // 2)
    sin_lo = jnp.where(first_half, -sin, 0.0)
    sin_hi = jnp.where(first_half, 0.0, sin)
    scale = HEAD_DIM ** -0.5 * math.log2(math.e)
    for c in range(D_ATTN // LANES):
        sl = slice(c * LANES, (c + 1) * LANES)
        qc = proj[:, c * LANES:(c + 1) * LANES]
        kc = proj[:, D_ATTN + c * LANES:D_ATTN + (c + 1) * LANES]
        q_ref[:, sl] = (_rope(qc, cos, sin_lo, sin_hi) * scale).astype(jnp.bfloat16)
        k_ref[:, sl] = _rope(kc, cos, sin_lo, sin_hi).astype(jnp.bfloat16)
    hw = 2 * HEAD_DIM
    for hd in range(N_DIFF_HEADS):
        vh = proj[:, 2 * D_ATTN + hd * hw:2 * D_ATTN + (hd + 1) * hw]
        vt_ref[0, hd, 0] = vh.T.astype(jnp.bfloat16)
    ua = proj[:, 3 * D_ATTN:3 * D_ATTN + D_CONV]
    ug = proj[:, 3 * D_ATTN + D_CONV:]
    z_ref[...] = ua * jax.nn.sigmoid(ug)


def _resident(shape):
    return pl.BlockSpec(shape, lambda i: (0,) * len(shape), pipeline_mode=pl.Buffered(1))


def _ffn1_proj(x2d, n1, wg, wu, wd, nm, win, invf, *, row_tile, rows_per_seq, pos_offset):
    rows = x2d.shape[0]
    assert rows % row_tile == 0 and rows_per_seq % row_tile == 0
    tiles_per_seq = rows_per_seq // row_tile
    hw = 2 * HEAD_DIM
    row_spec = lambda w: pl.BlockSpec((row_tile, w), lambda i: (i, 0))
    vt_spec = pl.BlockSpec((1, N_DIFF_HEADS, 1, hw, row_tile),
                           lambda i: (i // tiles_per_seq, 0, i % tiles_per_seq, 0, 0))
    kern = functools.partial(_ffn1_proj_kernel, tiles_per_seq=tiles_per_seq, pos_offset=pos_offset)
    return pl.pallas_call(
        kern,
        grid=(rows // row_tile,),
        in_specs=[row_spec(D_MODEL), _resident((1, D_MODEL)),
                  _resident((D_MODEL, D_FF)), _resident((D_MODEL, D_FF)), _resident((D_FF, D_MODEL)),
                  _resident((1, D_MODEL)), _resident((D_MODEL, D_IN_PROJ)), _resident((1, LANES))],
        out_specs=[row_spec(D_MODEL), row_spec(D_ATTN), row_spec(D_ATTN), vt_spec, row_spec(D_CONV)],
        out_shape=[jax.ShapeDtypeStruct((rows, D_MODEL), jnp.float32),
                   jax.ShapeDtypeStruct((rows, D_ATTN), jnp.bfloat16),
                   jax.ShapeDtypeStruct((rows, D_ATTN), jnp.bfloat16),
                   jax.ShapeDtypeStruct((rows // rows_per_seq, N_DIFF_HEADS, tiles_per_seq, hw, row_tile),
                                        jnp.bfloat16),
                   jax.ShapeDtypeStruct((rows, D_CONV), jnp.float32)],
        compiler_params=pltpu.CompilerParams(dimension_semantics=("arbitrary",),
                                             vmem_limit_bytes=VMEM_LIMIT_BYTES),
        name="ffn1_proj",
    )(x2d, n1, wg, wu, wd, nm, win, invf)


def _attn_kernel(q_ref, k_ref, vt_ref, km_ref, vmt_ref, lq1_ref, lk1_ref, lq2_ref, lk2_ref, sw_ref,
                 o_ref, m_scr, acc_scr, s0_scr, s1_scr):
    tq = q_ref.shape[1]
    hw, tk = vt_ref.shape[-2:]
    qi = pl.program_id(2)
    nt = (((1,), (1,)), ((), ()))
    n_chunks = 2 * tq // Q_CHUNK

    q = q_ref[0]
    lane = lax.broadcasted_iota(jnp.int32, q.shape, 1)
    zero = jnp.zeros_like(q)
    qs = jnp.concatenate([jnp.where(lane < HEAD_DIM, q, zero), jnp.where(lane >= HEAD_DIM, q, zero)], axis=0)
    q_chunks = [qs[c * Q_CHUNK:(c + 1) * Q_CHUNK] for c in range(n_chunks)]

    def with_ones(vt):
        return jnp.concatenate([vt, jnp.ones((ONES_ROWS, vt.shape[1]), vt.dtype)], axis=0)

    km = km_ref[0:N_META, :]
    pad = jnp.zeros((vmt_ref.shape[-1] - N_META, Q_CHUNK), jnp.bfloat16)
    vmt = with_ones(vmt_ref[0, 0, 0])
    for c in range(n_chunks):
        sl = slice(c * Q_CHUNK, (c + 1) * Q_CHUNK)
        s = lax.dot_general(km, q_chunks[c], nt, preferred_element_type=jnp.float32)
        m = jnp.max(s, axis=0, keepdims=True)
        p_pad = jnp.concatenate([jnp.exp2(s - m).astype(jnp.bfloat16), pad], axis=0)
        m_scr[:, sl] = m
        acc_scr[:, sl] = jnp.dot(vmt, p_pad, preferred_element_type=jnp.float32)

    def scores(j, s_scr):
        kb = k_ref[0, pl.ds(pl.multiple_of(j * tk, tk), tk), :]
        s_scr[:, 0:2 * tq] = lax.dot_general(kb, qs, nt, preferred_element_type=jnp.float32)

    def softmax_pv(j, s_scr, masked):
        vtb = with_ones(vt_ref[0, 0, j])
        for c in range(n_chunks):
            sl = slice(c * Q_CHUNK, (c + 1) * Q_CHUNK)
            if masked:
                q0 = (c * Q_CHUNK) % tq
                nk = min(tk, q0 + Q_CHUNK)
                s = s_scr[0:nk, sl]
                key = lax.broadcasted_iota(jnp.int32, s.shape, 0)
                qry = lax.broadcasted_iota(jnp.int32, s.shape, 1) + q0
                s = jnp.where(key <= qry, s, MASK_VALUE)
                vtc = vtb[:, 0:nk]
            else:
                s = s_scr[:, sl]
                vtc = vtb
            m_prev = m_scr[:, sl]
            m_new = jnp.maximum(m_prev, jnp.max(s, axis=0, keepdims=True))
            alpha = jnp.exp2(m_prev - m_new)
            p = jnp.exp2(s - m_new).astype(jnp.bfloat16)
            acc_scr[:, sl] = alpha * acc_scr[:, sl] + jnp.dot(vtc, p, preferred_element_type=jnp.float32)
            m_scr[:, sl] = m_new

    scores(0, s0_scr)

    def pair(p, carry):
        j = 2 * p
        scores(j + 1, s1_scr)
        softmax_pv(j, s0_scr, masked=False)
        scores(j + 2, s0_scr)
        softmax_pv(j + 1, s1_scr, masked=False)
        return carry

    lax.fori_loop(0, qi // 2, pair, 0)

    @pl.when(qi % 2 == 1)
    def _():
        scores(qi, s1_scr)
        softmax_pv(qi - 1, s0_scr, masked=False)
        softmax_pv(qi, s1_scr, masked=True)

    @pl.when(qi % 2 == 0)
    def _():
        softmax_pv(qi, s0_scr, masked=True)

    lam = (jnp.exp(jnp.sum(lq1_ref[...] * lk1_ref[...], axis=-1, keepdims=True))
           - jnp.exp(jnp.sum(lq2_ref[...] * lk2_ref[...], axis=-1, keepdims=True)) + LAMBDA_INIT)
    o = acc_scr[0:hw, :] / acc_scr[hw:hw + 1, :]
    o = o[:, :tq] - lam * o[:, tq:]
    o = o * lax.rsqrt(jnp.mean(o * o, axis=0, keepdims=True) + NORM_EPS)
    o_ref[0] = (o.T * sw_ref[...] * (1.0 - LAMBDA_INIT)).astype(o_ref.dtype)


def _diff_attention(q, k, vt, km, vmt, lq1, lk1, lq2, lk2, sw):
    b, t, _ = q.shape
    n_kv, hw, tk = vt.shape[2:]
    tq = tk
    assert t == n_kv * tk and (2 * tq) % Q_CHUNK == 0 and tq % Q_CHUNK == 0
    small = lambda shape: pl.BlockSpec(shape, lambda bi, hi, qi: (0,) * len(shape))
    return pl.pallas_call(
        _attn_kernel,
        grid=(b, N_DIFF_HEADS, t // tq),
        in_specs=[pl.BlockSpec((1, tq, hw), lambda bi, hi, qi: (bi, qi, hi)),
                  pl.BlockSpec((1, t, hw), lambda bi, hi, qi: (bi, 0, hi)),
                  pl.BlockSpec((1, 1, n_kv, hw, tk), lambda bi, hi, qi: (bi, hi, 0, 0, 0)),
                  pl.BlockSpec((km.shape[0], hw), lambda bi, hi, qi: (0, hi)),
                  pl.BlockSpec((1, 1, 1) + vmt.shape[3:], lambda bi, hi, qi: (0, hi, 0, 0, 0)),
                  small((1, HEAD_DIM)), small((1, HEAD_DIM)), small((1, HEAD_DIM)), small((1, HEAD_DIM)),
                  small((1, hw))],
        out_specs=pl.BlockSpec((1, tq, hw), lambda bi, hi, qi: (bi, qi, hi)),
        out_shape=jax.ShapeDtypeStruct((b, t, D_ATTN), jnp.bfloat16),
        scratch_shapes=[pltpu.VMEM((1, 2 * tq), jnp.float32),
                        pltpu.VMEM((hw + ONES_ROWS, 2 * tq), jnp.float32),
                        pltpu.VMEM((tk, 2 * tq + SCORE_PAD), jnp.float32),
                        pltpu.VMEM((tk, 2 * tq + SCORE_PAD), jnp.float32)],
        compiler_params=pltpu.CompilerParams(dimension_semantics=("arbitrary", "arbitrary", "arbitrary"),
                                             vmem_limit_bytes=VMEM_LIMIT_BYTES),
        name="diff_attn",
    )(q, k, vt, km, vmt, lq1, lk1, lq2, lk2, sw)


def _mix_ffn2_kernel(x1_ref, a_ref, z0_ref, znext_ref, zprev_ref, zstart_ref, zero_ref, cw_ref, cb_ref, lg_ref, lb_ref,
                     wo_ref, n2_ref, wg_ref, wu_ref, wd_ref, nf_ref, y_ref, zw_scr, c_scr, *, tiles_per_seq):
    tr = x1_ref.shape[0]
    i = pl.program_id(0)

    def fill_window(z_tile_ref, halo):
        zw_scr[0:HALO, :] = halo
        zw_scr[HALO:HALO + tr, :] = z_tile_ref[...]
        zw_scr[HALO + tr:, :] = jnp.zeros((SUBLANES, D_CONV), jnp.float32)

    def conv_rows(r0):
        base = HALO - (CONV_WIDTH - 1)
        conv = None
        for rho in range(SUBLANES):
            group = None
            for o in range(rho, HALO + 1, SUBLANES):
                if o < base:
                    continue
                lo = r0 + o - rho
                term = cw_ref[o - base:o - base + 1, :] * zw_scr[lo:lo + CONV_ROWS + SUBLANES, :]
                group = term if group is None else group + term
            shifted = group[rho:rho + CONV_ROWS]
            conv = shifted if conv is None else conv + shifted
        conv = conv + cb_ref[...]
        mu = jnp.mean(conv, axis=-1, keepdims=True)
        cc = conv - mu
        var = jnp.mean(cc * cc, axis=-1, keepdims=True)
        c = cc * lax.rsqrt(var + NORM_EPS) * lg_ref[...] + lb_ref[...]
        c = c * jax.nn.sigmoid(c)
        c_scr[r0:r0 + CONV_ROWS, :] = c.astype(jnp.bfloat16)
        bits = pltpu.bitcast(c[0:SUBLANES, 0:FF_CHUNK], jnp.int32) & zero_ref[...]
        return pltpu.bitcast(bits, jnp.float32)

    conv_starts = list(range(0, tr, CONV_ROWS))

    @pl.when(i == 0)
    def _():
        fill_window(z0_ref, zstart_ref[...])
        for r0 in conv_starts:
            conv_rows(r0)

    x2 = (x1_ref[...]
          + jnp.dot(a_ref[...], wo_ref[0:D_ATTN, :], preferred_element_type=jnp.float32)
          + jnp.dot(c_scr[...], wo_ref[D_ATTN:, :], preferred_element_type=jnp.float32))
    h = _rmsnorm(x2, n2_ref[...]).astype(jnp.bfloat16)

    next_starts_seq = ((i + 1) % tiles_per_seq) == 0
    fill_window(znext_ref, jnp.where(next_starts_seq, zstart_ref[...], zprev_ref[...]))
    ffn = None
    tokens = {}
    for ci, c0 in enumerate(range(0, D_FF, FF_CHUNK)):
        cols = slice(c0, c0 + FF_CHUNK)
        g = jnp.dot(h, wg_ref[:, cols], preferred_element_type=jnp.float32)
        u = jnp.dot(h, wu_ref[:, cols], preferred_element_type=jnp.float32)
        if ci in tokens:
            g = jnp.concatenate([g[0:SUBLANES] + tokens[ci], g[SUBLANES:]], axis=0)
        act = (g * jax.nn.sigmoid(g) * u).astype(jnp.bfloat16)
        d = jnp.dot(act, wd_ref[cols, :], preferred_element_type=jnp.float32)
        ffn = d if ffn is None else ffn + d
        if ci < len(conv_starts):
            tokens[ci + TOKEN_LAG] = conv_rows(conv_starts[ci])
    x3 = x2 + 0.5 * ffn
    y_ref[...] = _rmsnorm(x3, nf_ref[...])


def _mix_ffn2(x1, a, z, zstart, cw, cb, lg, lb, wo, n2, wg, wu, wd, nf, *, rows_per_seq):
    rows = x1.shape[0]
    tr = ROW_TILE
    assert rows % tr == 0 and rows_per_seq % tr == 0 and tr % HALO == 0
    assert D_FF % FF_CHUNK == 0 and tr % CONV_ROWS == 0 and D_FF // FF_CHUNK >= tr // CONV_ROWS + TOKEN_LAG
    n_tiles = rows // tr
    row_spec = lambda w: pl.BlockSpec((tr, w), lambda i: (i, 0))
    z0_spec = pl.BlockSpec((tr, D_CONV), lambda i: (0, 0), pipeline_mode=pl.Buffered(1))
    znext_spec = pl.BlockSpec((tr, D_CONV), lambda i: (jnp.minimum(i + 1, n_tiles - 1), 0))
    halo_spec = pl.BlockSpec((HALO, D_CONV), lambda i: ((i + 1) * (tr // HALO) - 1, 0))
    kern = functools.partial(_mix_ffn2_kernel, tiles_per_seq=rows_per_seq // tr)
    return pl.pallas_call(
        kern,
        grid=(n_tiles,),
        in_specs=[row_spec(D_MODEL), row_spec(D_ATTN), z0_spec, znext_spec, halo_spec,
                  _resident((HALO, D_CONV)), _resident((SUBLANES, FF_CHUNK)),
                  _resident((CONV_WIDTH, D_CONV)), _resident((1, D_CONV)),
                  _resident((1, D_CONV)), _resident((1, D_CONV)),
                  _resident((D_ATTN + D_CONV, D_MODEL)), _resident((1, D_MODEL)),
                  _resident((D_MODEL, D_FF)), _resident((D_MODEL, D_FF)), _resident((D_FF, D_MODEL)),
                  _resident((1, D_MODEL))],
        out_specs=row_spec(D_MODEL),
        out_shape=jax.ShapeDtypeStruct((rows, D_MODEL), jnp.float32),
        scratch_shapes=[pltpu.VMEM((HALO + tr + SUBLANES, D_CONV), jnp.float32),
                        pltpu.VMEM((tr, D_CONV), jnp.bfloat16)],
        compiler_params=pltpu.CompilerParams(dimension_semantics=("arbitrary",),
                                             vmem_limit_bytes=VMEM_LIMIT_BYTES),
        name="mix_ffn2",
    )(x1, a, z, z, z, zstart, jnp.zeros((SUBLANES, FF_CHUNK), jnp.int32), cw, cb, lg, lb, wo, n2, wg, wu, wd, nf)


def kernel(x, meta_tokens, ffn1_norm, ffn1_w_gate, ffn1_w_up, ffn1_w_down, mix_norm, w_in, lambda_q1, lambda_k1, lambda_q2, lambda_k2, subln_w, conv_w, conv_b, conv_ln_g, conv_ln_b, w_out, ffn2_norm, ffn2_w_gate, ffn2_w_up, ffn2_w_down, final_norm):
    b, t, d = x.shape
    bf16 = jnp.bfloat16
    row = lambda v: v.reshape(1, -1)

    inv_freq = ROPE_THETA ** (-jnp.arange(0, HEAD_DIM, 2, dtype=jnp.float32) / HEAD_DIM)
    invf = jnp.tile(inv_freq, LANES // (HEAD_DIM // 2)).reshape(1, LANES)

    ffn1_args = (row(ffn1_norm[0]), ffn1_w_gate[0].astype(bf16), ffn1_w_up[0].astype(bf16),
                 ffn1_w_down[0].astype(bf16), row(mix_norm[0]), w_in[0].astype(bf16), invf)
    x1, q, k, vt, z = _ffn1_proj(x.reshape(b * t, d), *ffn1_args,
                                 row_tile=ROW_TILE, rows_per_seq=t, pos_offset=N_META)
    meta = jnp.concatenate([meta_tokens, jnp.zeros((META_TILE - N_META, d), meta_tokens.dtype)], axis=0)
    _, _, km, vmt, zm = _ffn1_proj(meta, *ffn1_args, row_tile=META_TILE, rows_per_seq=META_TILE, pos_offset=0)

    a = _diff_attention(q.reshape(b, t, D_ATTN), k.reshape(b, t, D_ATTN), vt, km, vmt,
                        row(lambda_q1[0]), row(lambda_k1[0]), row(lambda_q2[0]), row(lambda_k2[0]),
                        row(subln_w[0]))

    zstart = jnp.concatenate([jnp.zeros((HALO - N_META, D_CONV), jnp.float32), zm[:N_META]], axis=0)
    y = _mix_ffn2(x1, a.reshape(b * t, D_ATTN), z, zstart, conv_w[0], row(conv_b[0]),
                  row(conv_ln_g[0]), row(conv_ln_b[0]), w_out[0].astype(bf16), row(ffn2_norm[0]),
                  ffn2_w_gate[0].astype(bf16), ffn2_w_up[0].astype(bf16), ffn2_w_down[0].astype(bf16),
                  row(final_norm), rows_per_seq=t)
    return y.reshape(b, t, d)
```

```python
import functools
import math

import jax
import jax.numpy as jnp
from jax import lax
from jax.experimental import pallas as pl
from jax.experimental.pallas import tpu as pltpu

D_MODEL = 1024
N_META = 16
D_ATTN = 512
D_CONV = 512
HEAD_DIM = 64
N_DIFF_HEADS = 4
CONV_WIDTH = 31
D_FF = 2816
ROPE_THETA = 10000.0
NORM_EPS = 1e-5
D_IN_PROJ = 3 * D_ATTN + 2 * D_CONV
LAMBDA_INIT = 0.8 - 0.6 * math.exp(-0.3 * 0)

LANES = 128
SUBLANES = 8
HALO = 32
ROW_TILE = 512
META_TILE = 128
Q_CHUNK = 256
FF_CHUNK = 256
CONV_ROWS = 64
SCORE_PAD = 128
TOKEN_LAG = 2
ONES_ROWS = 16
VMEM_LIMIT_BYTES = 56 * 1024 * 1024
MASK_VALUE = -1e30


def _rmsnorm(x, g):
    return x * lax.rsqrt(jnp.mean(x * x, axis=-1, keepdims=True) + NORM_EPS) * g


def _swiglu(h_bf16, wg_ref, wu_ref, wd_ref):
    g = jnp.dot(h_bf16, wg_ref[...], preferred_element_type=jnp.float32)
    u = jnp.dot(h_bf16, wu_ref[...], preferred_element_type=jnp.float32)
    a = (g * jax.nn.sigmoid(g) * u).astype(jnp.bfloat16)
    return jnp.dot(a, wd_ref[...], preferred_element_type=jnp.float32)


def _rope(x, cos, sin_lo, sin_hi):
    return x * cos + pltpu.roll(x, 96, 1) * sin_lo + pltpu.roll(x, 32, 1) * sin_hi


def _ffn1_proj_kernel(x_ref, n1_ref, wg_ref, wu_ref, wd_ref, nm_ref, win_ref, invf_ref,
                      x1_ref, q_ref, k_ref, vt_ref, z_ref, *, tiles_per_seq, pos_offset):
    tr = x_ref.shape[0]
    x = x_ref[...]
    h = _rmsnorm(x, n1_ref[...]).astype(jnp.bfloat16)
    x1 = x + 0.5 * _swiglu(h, wg_ref, wu_ref, wd_ref)
    x1_ref[...] = x1

    hm = _rmsnorm(x1, nm_ref[...]).astype(jnp.bfloat16)
    proj = jnp.dot(hm, win_ref[...], preferred_element_type=jnp.float32)

    t0 = (pl.program_id(0) % tiles_per_seq) * tr + pos_offset
    pos = (t0 + lax.broadcasted_iota(jnp.int32, (tr, LANES), 0)).astype(jnp.float32)
    ang = pos * invf_ref[...]
    cos = jnp.cos(ang)
    sin = jnp.sin(ang)
    lane = lax.broadcasted_iota(jnp.int32, (tr, LANES), 1)
    first_half = (lane % HEAD_DIM) < (HEAD_DIM // 2)
    sin_lo = jnp.where(first_half, -sin, 0.0)
    sin_hi = jnp.where(first_half, 0.0, sin)
    scale = HEAD_DIM ** -0.5 * math.log2(math.e)
    for c in range(D_ATTN // LANES):
        sl = slice(c * LANES, (c + 1) * LANES)
        qc = proj[:, c * LANES:(c + 1) * LANES]
        kc = proj[:, D_ATTN + c * LANES:D_ATTN + (c + 1) * LANES]
        q_ref[:, sl] = (_rope(qc, cos, sin_lo, sin_hi) * scale).astype(jnp.bfloat16)
        k_ref[:, sl] = _rope(kc, cos, sin_lo, sin_hi).astype(jnp.bfloat16)
    hw = 2 * HEAD_DIM
    for hd in range(N_DIFF_HEADS):
        vh = proj[:, 2 * D_ATTN + hd * hw:2 * D_ATTN + (hd + 1) * hw]
        vt_ref[0, hd, 0] = vh.T.astype(jnp.bfloat16)
    ua = proj[:, 3 * D_ATTN:3 * D_ATTN + D_CONV]
    ug = proj[:, 3 * D_ATTN + D_CONV:]
    z_ref[...] = ua * jax.nn.sigmoid(ug)


def _resident(shape):
    return pl.BlockSpec(shape, lambda i: (0,) * len(shape), pipeline_mode=pl.Buffered(1))


def _ffn1_proj(x2d, n1, wg, wu, wd, nm, win, invf, *, row_tile, rows_per_seq, pos_offset):
    rows = x2d.shape[0]
    assert rows % row_tile == 0 and rows_per_seq % row_tile == 0
    tiles_per_seq = rows_per_seq // row_tile
    hw = 2 * HEAD_DIM
    row_spec = lambda w: pl.BlockSpec((row_tile, w), lambda i: (i, 0))
    vt_spec = pl.BlockSpec((1, N_DIFF_HEADS, 1, hw, row_tile),
                           lambda i: (i // tiles_per_seq, 0, i % tiles_per_seq, 0, 0))
    kern = functools.partial(_ffn1_proj_kernel, tiles_per_seq=tiles_per_seq, pos_offset=pos_offset)
    return pl.pallas_call(
        kern,
        grid=(rows // row_tile,),
        in_specs=[row_spec(D_MODEL), _resident((1, D_MODEL)),
                  _resident((D_MODEL, D_FF)), _resident((D_MODEL, D_FF)), _resident((D_FF, D_MODEL)),
                  _resident((1, D_MODEL)), _resident((D_MODEL, D_IN_PROJ)), _resident((1, LANES))],
        out_specs=[row_spec(D_MODEL), row_spec(D_ATTN), row_spec(D_ATTN), vt_spec, row_spec(D_CONV)],
        out_shape=[jax.ShapeDtypeStruct((rows, D_MODEL), jnp.float32),
                   jax.ShapeDtypeStruct((rows, D_ATTN), jnp.bfloat16),
                   jax.ShapeDtypeStruct((rows, D_ATTN), jnp.bfloat16),
                   jax.ShapeDtypeStruct((rows // rows_per_seq, N_DIFF_HEADS, tiles_per_seq, hw, row_tile),
                                        jnp.bfloat16),
                   jax.ShapeDtypeStruct((rows, D_CONV), jnp.float32)],
        compiler_params=pltpu.CompilerParams(dimension_semantics=("arbitrary",),
                                             vmem_limit_bytes=VMEM_LIMIT_BYTES),
        name="ffn1_proj",
    )(x2d, n1, wg, wu, wd, nm, win, invf)


def _attn_kernel(q_ref, k_ref, vt_ref, km_ref, vmt_ref, lq1_ref, lk1_ref, lq2_ref, lk2_ref, sw_ref,
                 o_ref, m_scr, acc_scr, s0_scr, s1_scr, p0_scr, p1_scr, a0_scr, a1_scr):
    tq = q_ref.shape[1]
    hw, tk = vt_ref.shape[-2:]
    qi = pl.program_id(2)
    nt = (((1,), (1,)), ((), ()))
    n_chunks = 2 * tq // Q_CHUNK

    q = q_ref[0]
    lane = lax.broadcasted_iota(jnp.int32, q.shape, 1)
    zero = jnp.zeros_like(q)
    qs = jnp.concatenate([jnp.where(lane < HEAD_DIM, q, zero), jnp.where(lane >= HEAD_DIM, q, zero)], axis=0)

    def with_ones(vt):
        return jnp.concatenate([vt, jnp.ones((ONES_ROWS, vt.shape[1]), vt.dtype)], axis=0)

    s_bufs, p_bufs, a_bufs = (s0_scr, s1_scr), (p0_scr, p1_scr), (a0_scr, a1_scr)

    def scores(j, par):
        kb = k_ref[0, pl.ds(pl.multiple_of(j * tk, tk), tk), :]
        s_bufs[par][:, 0:2 * tq] = lax.dot_general(kb, qs, nt, preferred_element_type=jnp.float32)

    def softmax(par, key_off=None):
        s_scr, p_scr, a_scr = s_bufs[par], p_bufs[par], a_bufs[par]
        for c in range(n_chunks):
            sl = slice(c * Q_CHUNK, (c + 1) * Q_CHUNK)
            rel = tk if key_off is None else (c * Q_CHUNK) % tq - key_off
            if rel <= -Q_CHUNK:
                p_scr[:, sl] = jnp.zeros((tk, Q_CHUNK), jnp.bfloat16)
                a_scr[:, sl] = jnp.ones((1, Q_CHUNK), jnp.float32)
                continue
            if rel >= tk:
                nk = tk
                s = s_scr[:, sl]
            else:
                nk = min(tk, rel + Q_CHUNK)
                s = s_scr[0:nk, sl]
                key = lax.broadcasted_iota(jnp.int32, s.shape, 0)
                qry = lax.broadcasted_iota(jnp.int32, s.shape, 1) + rel
                s = jnp.where(key <= qry, s, MASK_VALUE)
                if nk < tk:
                    p_scr[nk:tk, sl] = jnp.zeros((tk - nk, Q_CHUNK), jnp.bfloat16)
            m_prev = m_scr[:, sl]
            m_new = jnp.maximum(m_prev, jnp.max(s, axis=0, keepdims=True))
            a_scr[:, sl] = jnp.exp2(m_prev - m_new)
            p_scr[0:nk, sl] = jnp.exp2(s - m_new).astype(jnp.bfloat16)
            m_scr[:, sl] = m_new

    def values(j, par):
        vtb = with_ones(vt_ref[0, 0, j])
        acc_scr[...] = a_bufs[par][...] * acc_scr[...] + jnp.dot(vtb, p_bufs[par][...],
                                                                preferred_element_type=jnp.float32)

    scores(0, 0)

    s = lax.dot_general(km_ref[0:N_META, :], qs, nt, preferred_element_type=jnp.float32)
    m = jnp.max(s, axis=0, keepdims=True)
    pad = jnp.zeros((vmt_ref.shape[-1] - N_META, 2 * tq), jnp.bfloat16)
    p_pad = jnp.concatenate([jnp.exp2(s - m).astype(jnp.bfloat16), pad], axis=0)
    m_scr[...] = m
    acc_scr[...] = jnp.dot(with_ones(vmt_ref[0, 0, 0]), p_pad, preferred_element_type=jnp.float32)

    @pl.when(qi == 0)
    def _():
        scores(1, 1)
        softmax(0, key_off=0)

    @pl.when(qi > 0)
    def _():
        scores(1, 1)
        softmax(0)

    def pair(p, carry):
        t = 2 * p + 1
        scores(t + 1, 0)
        softmax(1)
        values(t - 1, 0)
        scores(t + 2, 1)
        softmax(0)
        values(t, 1)
        return carry

    lax.fori_loop(0, jnp.maximum(qi - 1, 0), pair, 0)

    @pl.when(qi > 0)
    def _():
        scores(2 * qi, 0)
        softmax(1)
        values(2 * qi - 2, 0)
        scores(2 * qi + 1, 1)
        softmax(0, key_off=0)
        values(2 * qi - 1, 1)

    softmax(1, key_off=tk)
    values(2 * qi, 0)
    values(2 * qi + 1, 1)

    lam = (jnp.exp(jnp.sum(lq1_ref[...] * lk1_ref[...], axis=-1, keepdims=True))
           - jnp.exp(jnp.sum(lq2_ref[...] * lk2_ref[...], axis=-1, keepdims=True)) + LAMBDA_INIT)
    o = acc_scr[0:hw, :] / acc_scr[hw:hw + 1, :]
    o = o[:, :tq] - lam * o[:, tq:]
    o = o * lax.rsqrt(jnp.mean(o * o, axis=0, keepdims=True) + NORM_EPS)
    o_ref[0] = (o.T * sw_ref[...] * (1.0 - LAMBDA_INIT)).astype(o_ref.dtype)


def _diff_attention(q, k, vt, km, vmt, lq1, lk1, lq2, lk2, sw):
    b, t, _ = q.shape
    n_kv, hw, tk = vt.shape[2:]
    tq = 2 * tk
    assert t == n_kv * tk and t % tq == 0 and tk % Q_CHUNK == 0
    small = lambda shape: pl.BlockSpec(shape, lambda bi, hi, qi: (0,) * len(shape))
    return pl.pallas_call(
        _attn_kernel,
        grid=(b, N_DIFF_HEADS, t // tq),
        in_specs=[pl.BlockSpec((1, tq, hw), lambda bi, hi, qi: (bi, qi, hi)),
                  pl.BlockSpec((1, t, hw), lambda bi, hi, qi: (bi, 0, hi)),
                  pl.BlockSpec((1, 1, n_kv, hw, tk), lambda bi, hi, qi: (bi, hi, 0, 0, 0)),
                  pl.BlockSpec((km.shape[0], hw), lambda bi, hi, qi: (0, hi)),
                  pl.BlockSpec((1, 1, 1) + vmt.shape[3:], lambda bi, hi, qi: (0, hi, 0, 0, 0)),
                  small((1, HEAD_DIM)), small((1, HEAD_DIM)), small((1, HEAD_DIM)), small((1, HEAD_DIM)),
                  small((1, hw))],
        out_specs=pl.BlockSpec((1, tq, hw), lambda bi, hi, qi: (bi, qi, hi)),
        out_shape=jax.ShapeDtypeStruct((b, t, D_ATTN), jnp.bfloat16),
        scratch_shapes=[pltpu.VMEM((1, 2 * tq), jnp.float32),
                        pltpu.VMEM((hw + ONES_ROWS, 2 * tq), jnp.float32),
                        pltpu.VMEM((tk, 2 * tq + SCORE_PAD), jnp.float32),
                        pltpu.VMEM((tk, 2 * tq + SCORE_PAD), jnp.float32),
                        pltpu.VMEM((tk, 2 * tq), jnp.bfloat16),
                        pltpu.VMEM((tk, 2 * tq), jnp.bfloat16),
                        pltpu.VMEM((1, 2 * tq), jnp.float32),
                        pltpu.VMEM((1, 2 * tq), jnp.float32)],
        compiler_params=pltpu.CompilerParams(dimension_semantics=("arbitrary", "arbitrary", "arbitrary"),
                                             vmem_limit_bytes=VMEM_LIMIT_BYTES),
        name="diff_attn",
    )(q, k, vt, km, vmt, lq1, lk1, lq2, lk2, sw)


def _mix_ffn2_kernel(x1_ref, a_ref, z0_ref, znext_ref, zprev_ref, zstart_ref, zero_ref, cw_ref, cb_ref, lg_ref, lb_ref,
                     wo_ref, n2_ref, wg_ref, wu_ref, wd_ref, nf_ref, y_ref, zw_scr, c_scr, *, tiles_per_seq):
    tr = x1_ref.shape[0]
    i = pl.program_id(0)

    def fill_window(z_tile_ref, halo):
        zw_scr[0:HALO, :] = halo
        zw_scr[HALO:HALO + tr, :] = z_tile_ref[...]
        zw_scr[HALO + tr:, :] = jnp.zeros((SUBLANES, D_CONV), jnp.float32)

    def conv_rows(r0):
        base = HALO - (CONV_WIDTH - 1)
        conv = None
        for rho in range(SUBLANES):
            group = None
            for o in range(rho, HALO + 1, SUBLANES):
                if o < base:
                    continue
                lo = r0 + o - rho
                term = cw_ref[o - base:o - base + 1, :] * zw_scr[lo:lo + CONV_ROWS + SUBLANES, :]
                group = term if group is None else group + term
            shifted = group[rho:rho + CONV_ROWS]
            conv = shifted if conv is None else conv + shifted
        conv = conv + cb_ref[...]
        mu = jnp.mean(conv, axis=-1, keepdims=True)
        cc = conv - mu
        var = jnp.mean(cc * cc, axis=-1, keepdims=True)
        c = cc * lax.rsqrt(var + NORM_EPS) * lg_ref[...] + lb_ref[...]
        c = c * jax.nn.sigmoid(c)
        c_scr[r0:r0 + CONV_ROWS, :] = c.astype(jnp.bfloat16)
        bits = pltpu.bitcast(c[0:SUBLANES, 0:FF_CHUNK], jnp.int32) & zero_ref[...]
        return pltpu.bitcast(bits, jnp.float32)

    conv_starts = list(range(0, tr, CONV_ROWS))

    @pl.when(i == 0)
    def _():
        fill_window(z0_ref, zstart_ref[...])
        for r0 in conv_starts:
            conv_rows(r0)

    x2 = (x1_ref[...]
          + jnp.dot(a_ref[...], wo_ref[0:D_ATTN, :], preferred_element_type=jnp.float32)
          + jnp.dot(c_scr[...], wo_ref[D_ATTN:, :], preferred_element_type=jnp.float32))
    h = _rmsnorm(x2, n2_ref[...]).astype(jnp.bfloat16)

    next_starts_seq = ((i + 1) % tiles_per_seq) == 0
    fill_window(znext_ref, jnp.where(next_starts_seq, zstart_ref[...], zprev_ref[...]))
    ffn = None
    tokens = {}
    for ci, c0 in enumerate(range(0, D_FF, FF_CHUNK)):
        cols = slice(c0, c0 + FF_CHUNK)
        g = jnp.dot(h, wg_ref[:, cols], preferred_element_type=jnp.float32)
        u = jnp.dot(h, wu_ref[:, cols], preferred_element_type=jnp.float32)
        if ci in tokens:
            g = jnp.concatenate([g[0:SUBLANES] + tokens[ci], g[SUBLANES:]], axis=0)
        act = (g * jax.nn.sigmoid(g) * u).astype(jnp.bfloat16)
        d = jnp.dot(act, wd_ref[cols, :], preferred_element_type=jnp.float32)
        ffn = d if ffn is None else ffn + d
        if ci < len(conv_starts):
            tokens[ci + TOKEN_LAG] = conv_rows(conv_starts[ci])
    x3 = x2 + 0.5 * ffn
    y_ref[...] = _rmsnorm(x3, nf_ref[...])


def _mix_ffn2(x1, a, z, zstart, cw, cb, lg, lb, wo, n2, wg, wu, wd, nf, *, rows_per_seq):
    rows = x1.shape[0]
    tr = ROW_TILE
    assert rows % tr == 0 and rows_per_seq % tr == 0 and tr % HALO == 0
    assert D_FF % FF_CHUNK == 0 and tr % CONV_ROWS == 0 and D_FF // FF_CHUNK >= tr // CONV_ROWS + TOKEN_LAG
    n_tiles = rows // tr
    row_spec = lambda w: pl.BlockSpec((tr, w), lambda i: (i, 0))
    z0_spec = pl.BlockSpec((tr, D_CONV), lambda i: (0, 0), pipeline_mode=pl.Buffered(1))
    znext_spec = pl.BlockSpec((tr, D_CONV), lambda i: (jnp.minimum(i + 1, n_tiles - 1), 0))
    halo_spec = pl.BlockSpec((HALO, D_CONV), lambda i: ((i + 1) * (tr // HALO) - 1, 0))
    kern = functools.partial(_mix_ffn2_kernel, tiles_per_seq=rows_per_seq // tr)
    return pl.pallas_call(
        kern,
        grid=(n_tiles,),
        in_specs=[row_spec(D_MODEL), row_spec(D_ATTN), z0_spec, znext_spec, halo_spec,
                  _resident((HALO, D_CONV)), _resident((SUBLANES, FF_CHUNK)),
                  _resident((CONV_WIDTH, D_CONV)), _resident((1, D_CONV)),
                  _resident((1, D_CONV)), _resident((1, D_CONV)),
                  _resident((D_ATTN + D_CONV, D_MODEL)), _resident((1, D_MODEL)),
                  _resident((D_MODEL, D_FF)), _resident((D_MODEL, D_FF)), _resident((D_FF, D_MODEL)),
                  _resident((1, D_MODEL))],
        out_specs=row_spec(D_MODEL),
        out_shape=jax.ShapeDtypeStruct((rows, D_MODEL), jnp.float32),
        scratch_shapes=[pltpu.VMEM((HALO + tr + SUBLANES, D_CONV), jnp.float32),
                        pltpu.VMEM((tr, D_CONV), jnp.bfloat16)],
        compiler_params=pltpu.CompilerParams(dimension_semantics=("arbitrary",),
                                             vmem_limit_bytes=VMEM_LIMIT_BYTES),
        name="mix_ffn2",
    )(x1, a, z, z, z, zstart, jnp.zeros((SUBLANES, FF_CHUNK), jnp.int32), cw, cb, lg, lb, wo, n2, wg, wu, wd, nf)


def kernel(x, meta_tokens, ffn1_norm, ffn1_w_gate, ffn1_w_up, ffn1_w_down, mix_norm, w_in, lambda_q1, lambda_k1, lambda_q2, lambda_k2, subln_w, conv_w, conv_b, conv_ln_g, conv_ln_b, w_out, ffn2_norm, ffn2_w_gate, ffn2_w_up, ffn2_w_down, final_norm):
    b, t, d = x.shape
    bf16 = jnp.bfloat16
    row = lambda v: v.reshape(1, -1)

    inv_freq = ROPE_THETA ** (-jnp.arange(0, HEAD_DIM, 2, dtype=jnp.float32) / HEAD_DIM)
    invf = jnp.tile(inv_freq, LANES // (HEAD_DIM // 2)).reshape(1, LANES)

    ffn1_args = (row(ffn1_norm[0]), ffn1_w_gate[0].astype(bf16), ffn1_w_up[0].astype(bf16),
                 ffn1_w_down[0].astype(bf16), row(mix_norm[0]), w_in[0].astype(bf16), invf)
    x1, q, k, vt, z = _ffn1_proj(x.reshape(b * t, d), *ffn1_args,
                                 row_tile=ROW_TILE, rows_per_seq=t, pos_offset=N_META)
    meta = jnp.concatenate([meta_tokens, jnp.zeros((META_TILE - N_META, d), meta_tokens.dtype)], axis=0)
    _, _, km, vmt, zm = _ffn1_proj(meta, *ffn1_args, row_tile=META_TILE, rows_per_seq=META_TILE, pos_offset=0)

    a = _diff_attention(q.reshape(b, t, D_ATTN), k.reshape(b, t, D_ATTN), vt, km, vmt,
                        row(lambda_q1[0]), row(lambda_k1[0]), row(lambda_q2[0]), row(lambda_k2[0]),
                        row(subln_w[0]))

    zstart = jnp.concatenate([jnp.zeros((HALO - N_META, D_CONV), jnp.float32), zm[:N_META]], axis=0)
    y = _mix_ffn2(x1, a.reshape(b * t, D_ATTN), z, zstart, conv_w[0], row(conv_b[0]),
                  row(conv_ln_g[0]), row(conv_ln_b[0]), w_out[0].astype(bf16), row(ffn2_norm[0]),
                  ffn2_w_gate[0].astype(bf16), ffn2_w_up[0].astype(bf16), ffn2_w_down[0].astype(bf16),
                  row(final_norm), rows_per_seq=t)
    return y.reshape(b, t, d)
```

```python
import functools
import math

import jax
import jax.numpy as jnp
from jax import lax
from jax.experimental import pallas as pl
from jax.experimental.pallas import tpu as pltpu

D_MODEL = 1024
N_META = 16
D_ATTN = 512
D_CONV = 512
HEAD_DIM = 64
N_DIFF_HEADS = 4
CONV_WIDTH = 31
D_FF = 2816
ROPE_THETA = 10000.0
NORM_EPS = 1e-5
D_IN_PROJ = 3 * D_ATTN + 2 * D_CONV
LAMBDA_INIT = 0.8 - 0.6 * math.exp(-0.3 * 0)

LANES = 128
SUBLANES = 8
HALO = 32
ROW_TILE = 512
META_TILE = 128
Q_CHUNK = 256
FF_CHUNK = 256
CONV_ROWS = 64
SCORE_PAD = 128
TOKEN_LAG = 2
ONES_ROWS = 16
VMEM_LIMIT_BYTES = 56 * 1024 * 1024
MASK_VALUE = -1e30


def _rmsnorm(x, g):
    return x * lax.rsqrt(jnp.mean(x * x, axis=-1, keepdims=True) + NORM_EPS) * g


def _swiglu(h_bf16, wg_ref, wu_ref, wd_ref):
    g = jnp.dot(h_bf16, wg_ref[...], preferred_element_type=jnp.float32)
    u = jnp.dot(h_bf16, wu_ref[...], preferred_element_type=jnp.float32)
    a = (g * jax.nn.sigmoid(g) * u).astype(jnp.bfloat16)
    return jnp.dot(a, wd_ref[...], preferred_element_type=jnp.float32)


def _rope(x, cos, sin_lo, sin_hi):
    return x * cos + pltpu.roll(x, 96, 1) * sin_lo + pltpu.roll(x, 32, 1) * sin_hi


def _ffn1_proj_kernel(x_ref, n1_ref, wg_ref, wu_ref, wd_ref, nm_ref, win_ref, invf_ref,
                      x1_ref, q_ref, k_ref, vt_ref, z_ref, cosr_scr, sinr_scr, *, tiles_per_seq, pos_offset):
    tr = x_ref.shape[0]

    @pl.when(pl.program_id(0) == 0)
    def _():
        r = lax.broadcasted_iota(jnp.int32, (tr, LANES), 0).astype(jnp.float32)
        cosr_scr[...] = jnp.cos(r * invf_ref[...])
        sinr_scr[...] = jnp.sin(r * invf_ref[...])

    x = x_ref[...]
    h = _rmsnorm(x, n1_ref[...]).astype(jnp.bfloat16)
    x1 = x + 0.5 * _swiglu(h, wg_ref, wu_ref, wd_ref)
    x1_ref[...] = x1

    hm = _rmsnorm(x1, nm_ref[...]).astype(jnp.bfloat16)
    proj = jnp.dot(hm, win_ref[...], preferred_element_type=jnp.float32)

    t0 =((pl.program_id(0) % tiles_per_seq) * tr + pos_offset).astype(jnp.float32)
    cos0 = jnp.cos(t0 * invf_ref[...])
    sin0 = jnp.sin(t0 * invf_ref[...])
    cos = cos0 * cosr_scr[...] - sin0 * sinr_scr[...]
    sin = sin0 * cosr_scr[...] + cos0 * sinr_scr[...]
    lane = lax.broadcasted_iota(jnp.int32, (tr, LANES), 1)
    first_half = (lane % HEAD_DIM) < (HEAD_DIM // 2)
    sin_lo = jnp.where(first_half, -sin, 0.0)
    sin_hi = jnp.where(first_half, 0.0, sin)
    scale = HEAD_DIM ** -0.5 * math.log2(math.e)
    for c in range(D_ATTN // LANES):
        sl = slice(c * LANES, (c + 1) * LANES)
        qc = proj[:, c * LANES:(c + 1) * LANES]
        kc = proj[:, D_ATTN + c * LANES:D_ATTN + (c + 1) * LANES]
        q_ref[:, sl] = (_rope(qc, cos, sin_lo, sin_hi) * scale).astype(jnp.bfloat16)
        k_ref[:, sl] = _rope(kc, cos, sin_lo, sin_hi).astype(jnp.bfloat16)
    hw = 2 * HEAD_DIM
    for hd in range(N_DIFF_HEADS):
        vh = proj[:, 2 * D_ATTN + hd * hw:2 * D_ATTN + (hd + 1) * hw]
        vt_ref[0, hd, 0] = vh.T.astype(jnp.bfloat16)
    ua = proj[:, 3 * D_ATTN:3 * D_ATTN + D_CONV]
    ug = proj[:, 3 * D_ATTN + D_CONV:]
    z_ref[...] = ua * jax.nn.sigmoid(ug)


def _resident(shape):
    return pl.BlockSpec(shape, lambda i: (0,) * len(shape), pipeline_mode=pl.Buffered(1))


def _ffn1_proj(x2d, n1, wg, wu, wd, nm, win, invf, *, row_tile, rows_per_seq, pos_offset):
    rows = x2d.shape[0]
    assert rows % row_tile == 0 and rows_per_seq % row_tile == 0
    tiles_per_seq = rows_per_seq // row_tile
    hw = 2 * HEAD_DIM
    row_spec = lambda w: pl.BlockSpec((row_tile, w), lambda i: (i, 0))
    vt_spec = pl.BlockSpec((1, N_DIFF_HEADS, 1, hw, row_tile),
                           lambda i: (i // tiles_per_seq, 0, i % tiles_per_seq, 0, 0))
    kern = functools.partial(_ffn1_proj_kernel, tiles_per_seq=tiles_per_seq, pos_offset=pos_offset)
    return pl.pallas_call(
        kern,
        grid=(rows // row_tile,),
        in_specs=[row_spec(D_MODEL), _resident((1, D_MODEL)),
                  _resident((D_MODEL, D_FF)), _resident((D_MODEL, D_FF)), _resident((D_FF, D_MODEL)),
                  _resident((1, D_MODEL)), _resident((D_MODEL, D_IN_PROJ)), _resident((1, LANES))],
        out_specs=[row_spec(D_MODEL), row_spec(D_ATTN), row_spec(D_ATTN), vt_spec, row_spec(D_CONV)],
        out_shape=[jax.ShapeDtypeStruct((rows, D_MODEL), jnp.float32),
                   jax.ShapeDtypeStruct((rows, D_ATTN), jnp.bfloat16),
                   jax.ShapeDtypeStruct((rows, D_ATTN), jnp.bfloat16),
                   jax.ShapeDtypeStruct((rows // rows_per_seq, N_DIFF_HEADS, tiles_per_seq, hw, row_tile),
                                        jnp.bfloat16),
                   jax.ShapeDtypeStruct((rows, D_CONV), jnp.float32)],
        scratch_shapes=[pltpu.VMEM((row_tile, LANES), jnp.float32),
                        pltpu.VMEM((row_tile, LANES), jnp.float32)],
        compiler_params=pltpu.CompilerParams(dimension_semantics=("arbitrary",),
                                             vmem_limit_bytes=VMEM_LIMIT_BYTES),
        name="ffn1_proj",
    )(x2d, n1, wg, wu, wd, nm, win, invf)


def _attn_kernel(q_ref, qn_ref, k_ref, vt_ref, km_ref, vmt_ref, lq1_ref, lk1_ref, lq2_ref, lk2_ref, sw_ref,
                 o_ref, m_scr, acc_scr, s0_scr, s1_scr, sn_scr, p0_scr, p1_scr, a0_scr, a1_scr):
    tq = q_ref.shape[1]
    hw, tk = vt_ref.shape[-2:]
    qi = pl.program_id(2)
    nt = (((1,), (1,)), ((), ()))
    n_chunks = 2 * tq // Q_CHUNK

    def stack_maps(q):
        lane = lax.broadcasted_iota(jnp.int32, q.shape, 1)
        zero = jnp.zeros_like(q)
        return jnp.concatenate([jnp.where(lane < HEAD_DIM, q, zero), jnp.where(lane >= HEAD_DIM, q, zero)], axis=0)

    qs = stack_maps(q_ref[0])

    def with_ones(vt):
        return jnp.concatenate([vt, jnp.ones((ONES_ROWS, vt.shape[1]), vt.dtype)], axis=0)

    s_bufs, p_bufs, a_bufs = (s0_scr, s1_scr), (p0_scr, p1_scr), (a0_scr, a1_scr)

    def scores_into(s_scr, j, q_stacked):
        kb = k_ref[0, pl.ds(pl.multiple_of(j * tk, tk), tk), :]
        s_scr[:, 0:2 * tq] = lax.dot_general(kb, q_stacked, nt, preferred_element_type=jnp.float32)

    def scores(j, par):
        scores_into(s_bufs[par], j, qs)

    def softmax(par, key_off=None, src=None):
        s_scr, p_scr, a_scr = (s_bufs[par] if src is None else src), p_bufs[par], a_bufs[par]
        for c in range(n_chunks):
            sl = slice(c * Q_CHUNK, (c + 1) * Q_CHUNK)
            rel = tk if key_off is None else (c * Q_CHUNK) % tq - key_off
            if rel <= -Q_CHUNK:
                p_scr[:, sl] = jnp.zeros((tk, Q_CHUNK), jnp.bfloat16)
                a_scr[:, sl] = jnp.ones((1, Q_CHUNK), jnp.float32)
                continue
            if rel >= tk:
                nk = tk
                load = lambda: s_scr[:, sl]
            else:
                nk = min(tk, rel + Q_CHUNK)
                if nk < tk:
                    p_scr[nk:tk, sl] = jnp.zeros((tk - nk, Q_CHUNK), jnp.bfloat16)

                def load(nk=nk, rel=rel):
                    s = s_scr[0:nk, sl]
                    key = lax.broadcasted_iota(jnp.int32, s.shape, 0)
                    qry = lax.broadcasted_iota(jnp.int32, s.shape, 1) + rel
                    return jnp.where(key <= qry, s, MASK_VALUE)
            m_prev = m_scr[:, sl]
            m_new = jnp.maximum(m_prev, jnp.max(load(), axis=0, keepdims=True))
            a_scr[:, sl] = jnp.exp2(m_prev - m_new)
            p_scr[0:nk, sl] = jnp.exp2(load() - m_new).astype(jnp.bfloat16)
            m_scr[:, sl] = m_new

    def values(j, par):
        vtb = with_ones(vt_ref[0, 0, j])
        acc_scr[...] = a_bufs[par][...] * acc_scr[...] + jnp.dot(vtb, p_bufs[par][...],
                                                                preferred_element_type=jnp.float32)

    def meta_init():
        s = lax.dot_general(km_ref[0:N_META, :], qs, nt, preferred_element_type=jnp.float32)
        m = jnp.max(s, axis=0, keepdims=True)
        pad = jnp.zeros((vmt_ref.shape[-1] - N_META, 2 * tq), jnp.bfloat16)
        p_pad = jnp.concatenate([jnp.exp2(s - m).astype(jnp.bfloat16), pad], axis=0)
        m_scr[...] = m
        acc_scr[...] = jnp.dot(with_ones(vmt_ref[0, 0, 0]), p_pad, preferred_element_type=jnp.float32)

    @pl.when(qi == 0)
    def _():
        scores_into(sn_scr, 0, qs)
        meta_init()
        scores(1, 1)
        softmax(0, key_off=0, src=sn_scr)

    @pl.when(qi > 0)
    def _():
        meta_init()
        scores(1, 1)
        softmax(0, src=sn_scr)

    def pair(p, carry):
        t = 2 * p + 1
        scores(t + 1, 0)
        softmax(1)
        values(t - 1, 0)
        scores(t + 2, 1)
        softmax(0)
        values(t, 1)
        return carry

    lax.fori_loop(0, jnp.maximum(qi - 1, 0), pair, 0)

    @pl.when(qi > 0)
    def _():
        scores(2 * qi, 0)
        softmax(1)
        values(2 * qi - 2, 0)
        scores(2 * qi + 1, 1)
        softmax(0, key_off=0)
        values(2 * qi - 1, 1)

    scores_into(sn_scr, 0, stack_maps(qn_ref[0]))
    softmax(1, key_off=tk)
    values(2 * qi, 0)
    values(2 * qi + 1, 1)

    lam = (jnp.exp(jnp.sum(lq1_ref[...] * lk1_ref[...], axis=-1, keepdims=True))
           - jnp.exp(jnp.sum(lq2_ref[...] * lk2_ref[...], axis=-1, keepdims=True)) + LAMBDA_INIT)
    o = acc_scr[0:hw, :] / acc_scr[hw:hw + 1, :]
    o = o[:, :tq] - lam * o[:, tq:]
    o = o * lax.rsqrt(jnp.mean(o * o, axis=0, keepdims=True) + NORM_EPS)
    o_ref[0] = (o.T * sw_ref[...] * (1.0 - LAMBDA_INIT)).astype(o_ref.dtype)


def _diff_attention(q, k, vt, km, vmt, lq1, lk1, lq2, lk2, sw):
    b, t, _ = q.shape
    n_kv, hw, tk = vt.shape[2:]
    tq = 2 * tk
    assert t == n_kv * tk and t % tq == 0 and tk % Q_CHUNK == 0
    small = lambda shape: pl.BlockSpec(shape, lambda bi, hi, qi: (0,) * len(shape))
    n_q = t // tq
    return pl.pallas_call(
        _attn_kernel,
        grid=(b, N_DIFF_HEADS, n_q),
        in_specs=[pl.BlockSpec((1, tq, hw), lambda bi, hi, qi: (bi, qi, hi)),
                  pl.BlockSpec((1, tq, hw), lambda bi, hi, qi: (bi, jnp.minimum(qi + 1, n_q - 1), hi)),
                  pl.BlockSpec((1, t, hw), lambda bi, hi, qi: (bi, 0, hi)),
                  pl.BlockSpec((1, 1, n_kv, hw, tk), lambda bi, hi, qi: (bi, hi, 0, 0, 0)),
                  pl.BlockSpec((km.shape[0], hw), lambda bi, hi, qi: (0, hi)),
                  pl.BlockSpec((1, 1, 1) + vmt.shape[3:], lambda bi, hi, qi: (0, hi, 0, 0, 0)),
                  small((1, HEAD_DIM)), small((1, HEAD_DIM)), small((1, HEAD_DIM)), small((1, HEAD_DIM)),
                  small((1, hw))],
        out_specs=pl.BlockSpec((1, tq, hw), lambda bi, hi, qi: (bi, qi, hi)),
        out_shape=jax.ShapeDtypeStruct((b, t, D_ATTN), jnp.bfloat16),
        scratch_shapes=[pltpu.VMEM((1, 2 * tq), jnp.float32),
                        pltpu.VMEM((hw + ONES_ROWS, 2 * tq), jnp.float32),
                        pltpu.VMEM((tk, 2 * tq + SCORE_PAD), jnp.float32),
                        pltpu.VMEM((tk, 2 * tq + SCORE_PAD), jnp.float32),
                        pltpu.VMEM((tk, 2 * tq + SCORE_PAD), jnp.float32),
                        pltpu.VMEM((tk, 2 * tq), jnp.bfloat16),
                        pltpu.VMEM((tk, 2 * tq), jnp.bfloat16),
                        pltpu.VMEM((1, 2 * tq), jnp.float32),
                        pltpu.VMEM((1, 2 * tq), jnp.float32)],
        compiler_params=pltpu.CompilerParams(dimension_semantics=("arbitrary", "arbitrary", "arbitrary"),
                                             vmem_limit_bytes=VMEM_LIMIT_BYTES),
        name="diff_attn",
    )(q, q, k, vt, km, vmt, lq1, lk1, lq2, lk2, sw)


def _mix_ffn2_kernel(x1_ref, a_ref, z0_ref, znext_ref, zprev_ref, zstart_ref, zero_ref, cw_ref, cb_ref, lg_ref, lb_ref,
                     wo_ref, n2_ref, wg_ref, wu_ref, wd_ref, nf_ref, y_ref, zw_scr, c_scr, *, tiles_per_seq):
    tr = x1_ref.shape[0]
    i = pl.program_id(0)

    def fill_window(z_tile_ref, halo):
        zw_scr[0:HALO, :] = halo
        zw_scr[HALO:HALO + tr, :] = z_tile_ref[...]
        zw_scr[HALO + tr:, :] = jnp.zeros((SUBLANES, D_CONV), jnp.float32)

    def conv_rows(r0):
        base = HALO - (CONV_WIDTH - 1)
        conv = None
        for rho in range(SUBLANES):
            group = None
            for o in range(rho, HALO + 1, SUBLANES):
                if o < base:
                    continue
                lo = r0 + o - rho
                term = cw_ref[o - base:o - base + 1, :] * zw_scr[lo:lo + CONV_ROWS + SUBLANES, :]
                group = term if group is None else group + term
            shifted = group[rho:rho + CONV_ROWS]
            conv = shifted if conv is None else conv + shifted
        conv = conv + cb_ref[...]
        mu = jnp.mean(conv, axis=-1, keepdims=True)
        cc = conv - mu
        var = jnp.mean(cc * cc, axis=-1, keepdims=True)
        c = cc * lax.rsqrt(var + NORM_EPS) * lg_ref[...] + lb_ref[...]
        c = c * jax.nn.sigmoid(c)
        c_scr[r0:r0 + CONV_ROWS, :] = c.astype(jnp.bfloat16)
        bits = pltpu.bitcast(c[0:SUBLANES, 0:FF_CHUNK], jnp.int32) & zero_ref[...]
        return pltpu.bitcast(bits, jnp.float32)

    conv_starts = list(range(0, tr, CONV_ROWS))

    @pl.when(i == 0)
    def _():
        fill_window(z0_ref, zstart_ref[...])
        for r0 in conv_starts:
            conv_rows(r0)

    x2 = (x1_ref[...]
          + jnp.dot(a_ref[...], wo_ref[0:D_ATTN, :], preferred_element_type=jnp.float32)
          + jnp.dot(c_scr[...], wo_ref[D_ATTN:, :], preferred_element_type=jnp.float32))
    h = _rmsnorm(x2, n2_ref[...]).astype(jnp.bfloat16)

    next_starts_seq = ((i + 1) % tiles_per_seq) == 0
    fill_window(znext_ref, jnp.where(next_starts_seq, zstart_ref[...], zprev_ref[...]))
    ffn = None
    tokens = {}
    for ci, c0 in enumerate(range(0, D_FF, FF_CHUNK)):
        cols = slice(c0, c0 + FF_CHUNK)
        g = jnp.dot(h, wg_ref[:, cols], preferred_element_type=jnp.float32)
        u = jnp.dot(h, wu_ref[:, cols], preferred_element_type=jnp.float32)
        if ci in tokens:
            g = jnp.concatenate([g[0:SUBLANES] + tokens[ci], g[SUBLANES:]], axis=0)
        act = (g * jax.nn.sigmoid(g) * u).astype(jnp.bfloat16)
        d = jnp.dot(act, wd_ref[cols, :], preferred_element_type=jnp.float32)
        ffn = d if ffn is None else ffn + d
        if ci < len(conv_starts):
            tokens[ci + TOKEN_LAG] = conv_rows(conv_starts[ci])
    x3 = x2 + 0.5 * ffn
    y_ref[...] = _rmsnorm(x3, nf_ref[...])


def _mix_ffn2(x1, a, z, zstart, cw, cb, lg, lb, wo, n2, wg, wu, wd, nf, *, rows_per_seq):
    rows = x1.shape[0]
    tr = ROW_TILE
    assert rows % tr == 0 and rows_per_seq % tr == 0 and tr % HALO == 0
    assert D_FF % FF_CHUNK == 0 and tr % CONV_ROWS == 0 and D_FF // FF_CHUNK >= tr // CONV_ROWS + TOKEN_LAG
    n_tiles = rows // tr
    row_spec = lambda w: pl.BlockSpec((tr, w), lambda i: (i, 0))
    z0_spec = pl.BlockSpec((tr, D_CONV), lambda i: (0, 0), pipeline_mode=pl.Buffered(1))
    znext_spec = pl.BlockSpec((tr, D_CONV), lambda i: (jnp.minimum(i + 1, n_tiles - 1), 0))
    halo_spec = pl.BlockSpec((HALO, D_CONV), lambda i: ((i + 1) * (tr // HALO) - 1, 0))
    kern = functools.partial(_mix_ffn2_kernel, tiles_per_seq=rows_per_seq // tr)
    return pl.pallas_call(
        kern,
        grid=(n_tiles,),
        in_specs=[row_spec(D_MODEL), row_spec(D_ATTN), z0_spec, znext_spec, halo_spec,
                  _resident((HALO, D_CONV)), _resident((SUBLANES, FF_CHUNK)),
                  _resident((CONV_WIDTH, D_CONV)), _resident((1, D_CONV)),
                  _resident((1, D_CONV)), _resident((1, D_CONV)),
                  _resident((D_ATTN + D_CONV, D_MODEL)), _resident((1, D_MODEL)),
                  _resident((D_MODEL, D_FF)), _resident((D_MODEL, D_FF)), _resident((D_FF, D_MODEL)),
                  _resident((1, D_MODEL))],
        out_specs=row_spec(D_MODEL),
        out_shape=jax.ShapeDtypeStruct((rows, D_MODEL), jnp.float32),
        scratch_shapes=[pltpu.VMEM((HALO + tr + SUBLANES, D_CONV), jnp.float32),
                        pltpu.VMEM((tr, D_CONV), jnp.bfloat16)],
        compiler_params=pltpu.CompilerParams(dimension_semantics=("arbitrary",),
                                             vmem_limit_bytes=VMEM_LIMIT_BYTES),
        name="mix_ffn2",
    )(x1, a, z, z, z, zstart, jnp.zeros((SUBLANES, FF_CHUNK), jnp.int32), cw, cb, lg, lb, wo, n2, wg, wu, wd, nf)


def kernel(x, meta_tokens, ffn1_norm, ffn1_w_gate, ffn1_w_up, ffn1_w_down, mix_norm, w_in, lambda_q1, lambda_k1, lambda_q2, lambda_k2, subln_w, conv_w, conv_b, conv_ln_g, conv_ln_b, w_out, ffn2_norm, ffn2_w_gate, ffn2_w_up, ffn2_w_down, final_norm):
    b, t, d = x.shape
    bf16 = jnp.bfloat16
    row = lambda v: v.reshape(1, -1)

    inv_freq = ROPE_THETA ** (-jnp.arange(0, HEAD_DIM, 2, dtype=jnp.float32) / HEAD_DIM)
    invf = jnp.tile(inv_freq, LANES // (HEAD_DIM // 2)).reshape(1, LANES)

    ffn1_args = (row(ffn1_norm[0]), ffn1_w_gate[0].astype(bf16), ffn1_w_up[0].astype(bf16),
                 ffn1_w_down[0].astype(bf16), row(mix_norm[0]), w_in[0].astype(bf16), invf)
    x1, q, k, vt, z = _ffn1_proj(x.reshape(b * t, d), *ffn1_args,
                                 row_tile=ROW_TILE, rows_per_seq=t, pos_offset=N_META)
    meta = jnp.concatenate([meta_tokens, jnp.zeros((META_TILE - N_META, d), meta_tokens.dtype)], axis=0)
    _, _, km, vmt, zm = _ffn1_proj(meta, *ffn1_args, row_tile=META_TILE, rows_per_seq=META_TILE, pos_offset=0)

    a = _diff_attention(q.reshape(b, t, D_ATTN), k.reshape(b, t, D_ATTN), vt, km, vmt,
                        row(lambda_q1[0]), row(lambda_k1[0]), row(lambda_q2[0]), row(lambda_k2[0]),
                        row(subln_w[0]))

    zstart = jnp.concatenate([jnp.zeros((HALO - N_META, D_CONV), jnp.float32), zm[:N_META]], axis=0)
    y = _mix_ffn2(x1, a.reshape(b * t, D_ATTN), z, zstart, conv_w[0], row(conv_b[0]),
                  row(conv_ln_g[0]), row(conv_ln_b[0]), w_out[0].astype(bf16), row(ffn2_norm[0]),
                  ffn2_w_gate[0].astype(bf16), ffn2_w_up[0].astype(bf16), ffn2_w_down[0].astype(bf16),
                  row(final_norm), rows_per_seq=t)
    return y.reshape(b, t, d)
```

```python
import functools
import math

import jax
import jax.numpy as jnp
from jax import lax
from jax.experimental import pallas as pl
from jax.experimental.pallas import tpu as pltpu

D_MODEL = 1024
N_META = 16
D_ATTN = 512
D_CONV = 512
HEAD_DIM = 64
N_DIFF_HEADS = 4
CONV_WIDTH = 31
D_FF = 2816
ROPE_THETA = 10000.0
NORM_EPS = 1e-5
D_IN_PROJ = 3 * D_ATTN + 2 * D_CONV
LAMBDA_INIT = 0.8 - 0.6 * math.exp(-0.3 * 0)

LANES = 128
SUBLANES = 8
HALO = 32
ROW_TILE = 512
META_TILE = 128
Q_CHUNK = 256
FF_CHUNK = 256
CONV_ROWS = 64
SCORE_PAD = 128
TOKEN_LAG = 2
ONES_ROWS = 16
VMEM_LIMIT_BYTES = 56 * 1024 * 1024
MASK_VALUE = -1e30


def _rmsnorm(x, g):
    return x * lax.rsqrt(jnp.mean(x * x, axis=-1, keepdims=True) + NORM_EPS) * g


def _swiglu(h_bf16, wg_ref, wu_ref, wd_ref):
    g = jnp.dot(h_bf16, wg_ref[...], preferred_element_type=jnp.float32)
    u = jnp.dot(h_bf16, wu_ref[...], preferred_element_type=jnp.float32)
    a = (g * jax.nn.sigmoid(g) * u).astype(jnp.bfloat16)
    return jnp.dot(a, wd_ref[...], preferred_element_type=jnp.float32)


def _rope(x, cos, sin_lo, sin_hi):
    return x * cos + pltpu.roll(x, 96, 1) * sin_lo + pltpu.roll(x, 32, 1) * sin_hi


def _ffn1_proj_kernel(x_ref, n1_ref, wg_ref, wu_ref, wd_ref, nm_ref, win_ref, invf_ref,
                      x1_ref, q_ref, k_ref, vt_ref, z_ref, cosr_scr, sinr_scr, *, tiles_per_seq, pos_offset):
    tr = x_ref.shape[0]

    @pl.when(pl.program_id(0) == 0)
    def _():
        r = lax.broadcasted_iota(jnp.int32, (tr, LANES), 0).astype(jnp.float32)
        cosr_scr[...] = jnp.cos(r * invf_ref[...])
        sinr_scr[...] = jnp.sin(r * invf_ref[...])

    x = x_ref[...]
    h = _rmsnorm(x, n1_ref[...]).astype(jnp.bfloat16)
    x1 = x + 0.5 * _swiglu(h, wg_ref, wu_ref, wd_ref)
    x1_ref[...] = x1

    hm = _rmsnorm(x1, nm_ref[...]).astype(jnp.bfloat16)
    proj = jnp.dot(hm, win_ref[...], preferred_element_type=jnp.float32)

    t0 =((pl.program_id(0) % tiles_per_seq) * tr + pos_offset).astype(jnp.float32)
    cos0 = jnp.cos(t0 * invf_ref[...])
    sin0 = jnp.sin(t0 * invf_ref[...])
    cos = cos0 * cosr_scr[...] - sin0 * sinr_scr[...]
    sin = sin0 * cosr_scr[...] + cos0 * sinr_scr[...]
    lane = lax.broadcasted_iota(jnp.int32, (tr, LANES), 1)
    first_half = (lane % HEAD_DIM) < (HEAD_DIM // 2)
    sin_lo = jnp.where(first_half, -sin, 0.0)
    sin_hi = jnp.where(first_half, 0.0, sin)
    scale = HEAD_DIM ** -0.5 * math.log2(math.e)
    for c in range(D_ATTN // LANES):
        sl = slice(c * LANES, (c + 1) * LANES)
        qc = proj[:, c * LANES:(c + 1) * LANES]
        kc = proj[:, D_ATTN + c * LANES:D_ATTN + (c + 1) * LANES]
        q_ref[:, sl] = (_rope(qc, cos, sin_lo, sin_hi) * scale).astype(jnp.bfloat16)
        k_ref[:, sl] = _rope(kc, cos, sin_lo, sin_hi).astype(jnp.bfloat16)
    hw = 2 * HEAD_DIM
    for hd in range(N_DIFF_HEADS):
        vh = proj[:, 2 * D_ATTN + hd * hw:2 * D_ATTN + (hd + 1) * hw]
        vt_ref[0, hd, 0] = vh.T.astype(jnp.bfloat16)
    ua = proj[:, 3 * D_ATTN:3 * D_ATTN + D_CONV]
    ug = proj[:, 3 * D_ATTN + D_CONV:]
    z_ref[...] = ua * jax.nn.sigmoid(ug)


def _resident(shape):
    return pl.BlockSpec(shape, lambda i: (0,) * len(shape), pipeline_mode=pl.Buffered(1))


def _ffn1_proj(x2d, n1, wg, wu, wd, nm, win, invf, *, row_tile, rows_per_seq, pos_offset):
    rows = x2d.shape[0]
    assert rows % row_tile == 0 and rows_per_seq % row_tile == 0
    tiles_per_seq = rows_per_seq // row_tile
    hw = 2 * HEAD_DIM
    row_spec = lambda w: pl.BlockSpec((row_tile, w), lambda i: (i, 0))
    vt_spec = pl.BlockSpec((1, N_DIFF_HEADS, 1, hw, row_tile),
                           lambda i: (i // tiles_per_seq, 0, i % tiles_per_seq, 0, 0))
    kern = functools.partial(_ffn1_proj_kernel, tiles_per_seq=tiles_per_seq, pos_offset=pos_offset)
    return pl.pallas_call(
        kern,
        grid=(rows // row_tile,),
        in_specs=[row_spec(D_MODEL), _resident((1, D_MODEL)),
                  _resident((D_MODEL, D_FF)), _resident((D_MODEL, D_FF)), _resident((D_FF, D_MODEL)),
                  _resident((1, D_MODEL)), _resident((D_MODEL, D_IN_PROJ)), _resident((1, LANES))],
        out_specs=[row_spec(D_MODEL), row_spec(D_ATTN), row_spec(D_ATTN), vt_spec, row_spec(D_CONV)],
        out_shape=[jax.ShapeDtypeStruct((rows, D_MODEL), jnp.float32),
                   jax.ShapeDtypeStruct((rows, D_ATTN), jnp.bfloat16),
                   jax.ShapeDtypeStruct((rows, D_ATTN), jnp.bfloat16),
                   jax.ShapeDtypeStruct((rows // rows_per_seq, N_DIFF_HEADS, tiles_per_seq, hw, row_tile),
                                        jnp.bfloat16),
                   jax.ShapeDtypeStruct((rows, D_CONV), jnp.float32)],
        scratch_shapes=[pltpu.VMEM((row_tile, LANES), jnp.float32),
                        pltpu.VMEM((row_tile, LANES), jnp.float32)],
        compiler_params=pltpu.CompilerParams(dimension_semantics=("arbitrary",),
                                             vmem_limit_bytes=VMEM_LIMIT_BYTES),
        name="ffn1_proj",
    )(x2d, n1, wg, wu, wd, nm, win, invf)


def _attn_kernel(q_ref, qn_ref, k_ref, vt_ref, km_ref, vmt_ref, lq1_ref, lk1_ref, lq2_ref, lk2_ref, sw_ref,
                 o_ref, m_scr, acc_scr, s0_scr, s1_scr, sn_scr, p0_scr, p1_scr, a0_scr, a1_scr,
                 c0_scr, c1_scr, cn_scr):
    tq = q_ref.shape[1]
    hw, tk = vt_ref.shape[-2:]
    qi = pl.program_id(2)
    nt = (((1,), (1,)), ((), ()))
    n_chunks = 2 * tq // Q_CHUNK

    def stack_maps(q):
        lane = lax.broadcasted_iota(jnp.int32, q.shape, 1)
        zero = jnp.zeros_like(q)
        return jnp.concatenate([jnp.where(lane < HEAD_DIM, q, zero), jnp.where(lane >= HEAD_DIM, q, zero)], axis=0)

    qs = stack_maps(q_ref[0])

    def with_ones(vt):
        return jnp.concatenate([vt, jnp.ones((ONES_ROWS, vt.shape[1]), vt.dtype)], axis=0)

    s_bufs, p_bufs, a_bufs = ((s0_scr, c0_scr), (s1_scr, c1_scr)), (p0_scr, p1_scr), (a0_scr, a1_scr)
    s_next = (sn_scr, cn_scr)

    def scores_into(bufs, j, q_stacked):
        s_scr, c_scr = bufs
        kb = k_ref[0, pl.ds(pl.multiple_of(j * tk, tk), tk), :]
        s = lax.dot_general(kb, q_stacked, nt, preferred_element_type=jnp.float32)
        s_scr[:, 0:2 * tq] = s
        c_scr[...] = jnp.max(s, axis=0, keepdims=True)

    def scores(j, par):
        scores_into(s_bufs[par], j, qs)

    def softmax(par, key_off=None, src=None):
        (s_scr, c_scr), p_scr, a_scr = (s_bufs[par] if src is None else src), p_bufs[par], a_bufs[par]
        for c in range(n_chunks):
            sl = slice(c * Q_CHUNK, (c + 1) * Q_CHUNK)
            rel = tk if key_off is None else (c * Q_CHUNK) % tq - key_off
            if rel <= -Q_CHUNK:
                p_scr[:, sl] = jnp.zeros((tk, Q_CHUNK), jnp.bfloat16)
                a_scr[:, sl] = jnp.ones((1, Q_CHUNK), jnp.float32)
                continue
            if rel >= tk:
                nk = tk
                load = lambda: s_scr[:, sl]
                col_max = c_scr[:, sl]
            else:
                nk = min(tk, rel + Q_CHUNK)
                if nk < tk:
                    p_scr[nk:tk, sl] = jnp.zeros((tk - nk, Q_CHUNK), jnp.bfloat16)

                def load(nk=nk, rel=rel):
                    s = s_scr[0:nk, sl]
                    key = lax.broadcasted_iota(jnp.int32, s.shape, 0)
                    qry = lax.broadcasted_iota(jnp.int32, s.shape, 1) + rel
                    return jnp.where(key <= qry, s, MASK_VALUE)
                col_max = jnp.max(load(), axis=0, keepdims=True)
            m_prev = m_scr[:, sl]
            m_new = jnp.maximum(m_prev, col_max)
            a_scr[:, sl] = jnp.exp2(m_prev - m_new)
            p_scr[0:nk, sl] = jnp.exp2(load() - m_new).astype(jnp.bfloat16)
            m_scr[:, sl] = m_new

    def values(j, par):
        vtb = with_ones(vt_ref[0, 0, j])
        acc_scr[...] = a_bufs[par][...] * acc_scr[...] + jnp.dot(vtb, p_bufs[par][...],
                                                                preferred_element_type=jnp.float32)

    def meta_init():
        s = lax.dot_general(km_ref[0:N_META, :], qs, nt, preferred_element_type=jnp.float32)
        m = jnp.max(s, axis=0, keepdims=True)
        pad = jnp.zeros((vmt_ref.shape[-1] - N_META, 2 * tq), jnp.bfloat16)
        p_pad = jnp.concatenate([jnp.exp2(s - m).astype(jnp.bfloat16), pad], axis=0)
        m_scr[...] = m
        acc_scr[...] = jnp.dot(with_ones(vmt_ref[0, 0, 0]), p_pad, preferred_element_type=jnp.float32)

    @pl.when(qi == 0)
    def _():
        scores_into(s_next, 0, qs)
        meta_init()
        scores(1, 1)
        softmax(0, key_off=0, src=s_next)

    @pl.when(qi > 0)
    def _():
        meta_init()
        scores(1, 1)
        softmax(0, src=s_next)

    def pair(p, carry):
        t = 2 * p + 1
        scores(t + 1, 0)
        softmax(1)
        values(t - 1, 0)
        scores(t + 2, 1)
        softmax(0)
        values(t, 1)
        return carry

    lax.fori_loop(0, jnp.maximum(qi - 1, 0), pair, 0)

    @pl.when(qi > 0)
    def _():
        scores(2 * qi, 0)
        softmax(1)
        values(2 * qi - 2, 0)
        scores(2 * qi + 1, 1)
        softmax(0, key_off=0)
        values(2 * qi - 1, 1)

    scores_into(s_next, 0, stack_maps(qn_ref[0]))
    softmax(1, key_off=tk)
    values(2 * qi, 0)
    values(2 * qi + 1, 1)

    lam = (jnp.exp(jnp.sum(lq1_ref[...] * lk1_ref[...], axis=-1, keepdims=True))
           - jnp.exp(jnp.sum(lq2_ref[...] * lk2_ref[...], axis=-1, keepdims=True)) + LAMBDA_INIT)
    o = acc_scr[0:hw, :] / acc_scr[hw:hw + 1, :]
    o = o[:, :tq] - lam * o[:, tq:]
    o = o * lax.rsqrt(jnp.mean(o * o, axis=0, keepdims=True) + NORM_EPS)
    o_ref[0] = (o.T * sw_ref[...] * (1.0 - LAMBDA_INIT)).astype(o_ref.dtype)


def _diff_attention(q, k, vt, km, vmt, lq1, lk1, lq2, lk2, sw):
    b, t, _ = q.shape
    n_kv, hw, tk = vt.shape[2:]
    tq = 2 * tk
    assert t == n_kv * tk and t % tq == 0 and tk % Q_CHUNK == 0
    small = lambda shape: pl.BlockSpec(shape, lambda bi, hi, qi: (0,) * len(shape))
    n_q = t // tq
    return pl.pallas_call(
        _attn_kernel,
        grid=(b, N_DIFF_HEADS, n_q),
        in_specs=[pl.BlockSpec((1, tq, hw), lambda bi, hi, qi: (bi, qi, hi)),
                  pl.BlockSpec((1, tq, hw), lambda bi, hi, qi: (bi, jnp.minimum(qi + 1, n_q - 1), hi)),
                  pl.BlockSpec((1, t, hw), lambda bi, hi, qi: (bi, 0, hi)),
                  pl.BlockSpec((1, 1, n_kv, hw, tk), lambda bi, hi, qi: (bi, hi, 0, 0, 0)),
                  pl.BlockSpec((km.shape[0], hw), lambda bi, hi, qi: (0, hi)),
                  pl.BlockSpec((1, 1, 1) + vmt.shape[3:], lambda bi, hi, qi: (0, hi, 0, 0, 0)),
                  small((1, HEAD_DIM)), small((1, HEAD_DIM)), small((1, HEAD_DIM)), small((1, HEAD_DIM)),
                  small((1, hw))],
        out_specs=pl.BlockSpec((1, tq, hw), lambda bi, hi, qi: (bi, qi, hi)),
        out_shape=jax.ShapeDtypeStruct((b, t, D_ATTN), jnp.bfloat16),
        scratch_shapes=[pltpu.VMEM((1, 2 * tq), jnp.float32),
                        pltpu.VMEM((hw + ONES_ROWS, 2 * tq), jnp.float32),
                        pltpu.VMEM((tk, 2 * tq + SCORE_PAD), jnp.float32),
                        pltpu.VMEM((tk, 2 * tq + SCORE_PAD), jnp.float32),
                        pltpu.VMEM((tk, 2 * tq + SCORE_PAD), jnp.float32),
                        pltpu.VMEM((tk, 2 * tq), jnp.bfloat16),
                        pltpu.VMEM((tk, 2 * tq), jnp.bfloat16),
                        pltpu.VMEM((1, 2 * tq), jnp.float32),
                        pltpu.VMEM((1, 2 * tq), jnp.float32),
                        pltpu.VMEM((1, 2 * tq), jnp.float32),
                        pltpu.VMEM((1, 2 * tq), jnp.float32),
                        pltpu.VMEM((1, 2 * tq), jnp.float32)],
        compiler_params=pltpu.CompilerParams(dimension_semantics=("arbitrary", "arbitrary", "arbitrary"),
                                             vmem_limit_bytes=VMEM_LIMIT_BYTES),
        name="diff_attn",
    )(q, q, k, vt, km, vmt, lq1, lk1, lq2, lk2, sw)


def _mix_ffn2_kernel(x1_ref, a_ref, z0_ref, znext_ref, zprev_ref, zstart_ref, zero_ref, cw_ref, cb_ref, lg_ref, lb_ref,
                     wo_ref, n2_ref, wg_ref, wu_ref, wd_ref, nf_ref, y_ref, zw_scr, c_scr, *, tiles_per_seq):
    tr = x1_ref.shape[0]
    i = pl.program_id(0)

    def fill_window(z_tile_ref, halo):
        zw_scr[0:HALO, :] = halo
        zw_scr[HALO:HALO + tr, :] = z_tile_ref[...]
        zw_scr[HALO + tr:, :] = jnp.zeros((SUBLANES, D_CONV), jnp.float32)

    def conv_rows(r0):
        base = HALO - (CONV_WIDTH - 1)
        conv = None
        for rho in range(SUBLANES):
            group = None
            for o in range(rho, HALO + 1, SUBLANES):
                if o < base:
                    continue
                lo = r0 + o - rho
                term = cw_ref[o - base:o - base + 1, :] * zw_scr[lo:lo + CONV_ROWS + SUBLANES, :]
                group = term if group is None else group + term
            shifted = group[rho:rho + CONV_ROWS]
            conv = shifted if conv is None else conv + shifted
        conv = conv + cb_ref[...]
        mu = jnp.mean(conv, axis=-1, keepdims=True)
        cc = conv - mu
        var = jnp.mean(cc * cc, axis=-1, keepdims=True)
        c = cc * lax.rsqrt(var + NORM_EPS) * lg_ref[...] + lb_ref[...]
        c = c * jax.nn.sigmoid(c)
        c_scr[r0:r0 + CONV_ROWS, :] = c.astype(jnp.bfloat16)
        bits = pltpu.bitcast(c[0:SUBLANES, 0:FF_CHUNK], jnp.int32) & zero_ref[...]
        return pltpu.bitcast(bits, jnp.float32)

    conv_starts = list(range(0, tr, CONV_ROWS))

    @pl.when(i == 0)
    def _():
        fill_window(z0_ref, zstart_ref[...])
        for r0 in conv_starts:
            conv_rows(r0)

    x2 = (x1_ref[...]
          + jnp.dot(a_ref[...], wo_ref[0:D_ATTN, :], preferred_element_type=jnp.float32)
          + jnp.dot(c_scr[...], wo_ref[D_ATTN:, :], preferred_element_type=jnp.float32))
    h = _rmsnorm(x2, n2_ref[...]).astype(jnp.bfloat16)

    next_starts_seq = ((i + 1) % tiles_per_seq) == 0
    fill_window(znext_ref, jnp.where(next_starts_seq, zstart_ref[...], zprev_ref[...]))
    ffn = None
    tokens = {}
    for ci, c0 in enumerate(range(0, D_FF, FF_CHUNK)):
        cols = slice(c0, c0 + FF_CHUNK)
        g = jnp.dot(h, wg_ref[:, cols], preferred_element_type=jnp.float32)
        u = jnp.dot(h, wu_ref[:, cols], preferred_element_type=jnp.float32)
        if ci in tokens:
            g = jnp.concatenate([g[0:SUBLANES] + tokens[ci], g[SUBLANES:]], axis=0)
        act = (g * jax.nn.sigmoid(g) * u).astype(jnp.bfloat16)
        d = jnp.dot(act, wd_ref[cols, :], preferred_element_type=jnp.float32)
        ffn = d if ffn is None else ffn + d
        if ci < len(conv_starts):
            tokens[ci + TOKEN_LAG] = conv_rows(conv_starts[ci])
    x3 = x2 + 0.5 * ffn
    y_ref[...] = _rmsnorm(x3, nf_ref[...])


def _mix_ffn2(x1, a, z, zstart, cw, cb, lg, lb, wo, n2, wg, wu, wd, nf, *, rows_per_seq):
    rows = x1.shape[0]
    tr = ROW_TILE
    assert rows % tr == 0 and rows_per_seq % tr == 0 and tr % HALO == 0
    assert D_FF % FF_CHUNK == 0 and tr % CONV_ROWS == 0 and D_FF // FF_CHUNK >= tr // CONV_ROWS + TOKEN_LAG
    n_tiles = rows // tr
    row_spec = lambda w: pl.BlockSpec((tr, w), lambda i: (i, 0))
    z0_spec = pl.BlockSpec((tr, D_CONV), lambda i: (0, 0), pipeline_mode=pl.Buffered(1))
    znext_spec = pl.BlockSpec((tr, D_CONV), lambda i: (jnp.minimum(i + 1, n_tiles - 1), 0))
    halo_spec = pl.BlockSpec((HALO, D_CONV), lambda i: ((i + 1) * (tr // HALO) - 1, 0))
    kern = functools.partial(_mix_ffn2_kernel, tiles_per_seq=rows_per_seq // tr)
    return pl.pallas_call(
        kern,
        grid=(n_tiles,),
        in_specs=[row_spec(D_MODEL), row_spec(D_ATTN), z0_spec, znext_spec, halo_spec,
                  _resident((HALO, D_CONV)), _resident((SUBLANES, FF_CHUNK)),
                  _resident((CONV_WIDTH, D_CONV)), _resident((1, D_CONV)),
                  _resident((1, D_CONV)), _resident((1, D_CONV)),
                  _resident((D_ATTN + D_CONV, D_MODEL)), _resident((1, D_MODEL)),
                  _resident((D_MODEL, D_FF)), _resident((D_MODEL, D_FF)), _resident((D_FF, D_MODEL)),
                  _resident((1, D_MODEL))],
        out_specs=row_spec(D_MODEL),
        out_shape=jax.ShapeDtypeStruct((rows, D_MODEL), jnp.float32),
        scratch_shapes=[pltpu.VMEM((HALO + tr + SUBLANES, D_CONV), jnp.float32),
                        pltpu.VMEM((tr, D_CONV), jnp.bfloat16)],
        compiler_params=pltpu.CompilerParams(dimension_semantics=("arbitrary",),
                                             vmem_limit_bytes=VMEM_LIMIT_BYTES),
        name="mix_ffn2",
    )(x1, a, z, z, z, zstart, jnp.zeros((SUBLANES, FF_CHUNK), jnp.int32), cw, cb, lg, lb, wo, n2, wg, wu, wd, nf)


def kernel(x, meta_tokens, ffn1_norm, ffn1_w_gate, ffn1_w_up, ffn1_w_down, mix_norm, w_in, lambda_q1, lambda_k1, lambda_q2, lambda_k2, subln_w, conv_w, conv_b, conv_ln_g, conv_ln_b, w_out, ffn2_norm, ffn2_w_gate, ffn2_w_up, ffn2_w_down, final_norm):
    b, t, d = x.shape
    bf16 = jnp.bfloat16
    row = lambda v: v.reshape(1, -1)

    inv_freq = ROPE_THETA ** (-jnp.arange(0, HEAD_DIM, 2, dtype=jnp.float32) / HEAD_DIM)
    invf = jnp.tile(inv_freq, LANES // (HEAD_DIM // 2)).reshape(1, LANES)

    ffn1_args = (row(ffn1_norm[0]), ffn1_w_gate[0].astype(bf16), ffn1_w_up[0].astype(bf16),
                 ffn1_w_down[0].astype(bf16), row(mix_norm[0]), w_in[0].astype(bf16), invf)
    x1, q, k, vt, z = _ffn1_proj(x.reshape(b * t, d), *ffn1_args,
                                 row_tile=ROW_TILE, rows_per_seq=t, pos_offset=N_META)
    meta = jnp.concatenate([meta_tokens, jnp.zeros((META_TILE - N_META, d), meta_tokens.dtype)], axis=0)
    _, _, km, vmt, zm = _ffn1_proj(meta, *ffn1_args, row_tile=META_TILE, rows_per_seq=META_TILE, pos_offset=0)

    a = _diff_attention(q.reshape(b, t, D_ATTN), k.reshape(b, t, D_ATTN), vt, km, vmt,
                        row(lambda_q1[0]), row(lambda_k1[0]), row(lambda_q2[0]), row(lambda_k2[0]),
                        row(subln_w[0]))

    zstart = jnp.concatenate([jnp.zeros((HALO - N_META, D_CONV), jnp.float32), zm[:N_META]], axis=0)
    y = _mix_ffn2(x1, a.reshape(b * t, D_ATTN), z, zstart, conv_w[0], row(conv_b[0]),
                  row(conv_ln_g[0]), row(conv_ln_b[0]), w_out[0].astype(bf16), row(ffn2_norm[0]),
                  ffn2_w_gate[0].astype(bf16), ffn2_w_up[0].astype(bf16), ffn2_w_down[0].astype(bf16),
                  row(final_norm), rows_per_seq=t)
    return y.reshape(b, t, d)
```

```python
import functools
import math

import jax
import jax.numpy as jnp
from jax import lax
from jax.experimental import pallas as pl
from jax.experimental.pallas import tpu as pltpu

D_MODEL = 1024
N_META = 16
D_ATTN = 512
D_CONV = 512
HEAD_DIM = 64
N_DIFF_HEADS = 4
CONV_WIDTH = 31
D_FF = 2816
ROPE_THETA = 10000.0
NORM_EPS = 1e-5
D_IN_PROJ = 3 * D_ATTN + 2 * D_CONV
LAMBDA_INIT = 0.8 - 0.6 * math.exp(-0.3 * 0)

LANES = 128
SUBLANES = 8
HALO = 32
ROW_TILE = 512
META_TILE = 128
Q_CHUNK = 256
FF_CHUNK = 256
CONV_ROWS = 64
SCORE_PAD = 128
TOKEN_LAG = 2
ONES_ROWS = 16
VMEM_LIMIT_BYTES = 56 * 1024 * 1024
MASK_VALUE = -1e30


def _rmsnorm(x, g):
    return x * lax.rsqrt(jnp.mean(x * x, axis=-1, keepdims=True) + NORM_EPS) * g


def _swiglu(h_bf16, wg_ref, wu_ref, wd_ref):
    g = jnp.dot(h_bf16, wg_ref[...], preferred_element_type=jnp.float32)
    u = jnp.dot(h_bf16, wu_ref[...], preferred_element_type=jnp.float32)
    a = (g * jax.nn.sigmoid(g) * u).astype(jnp.bfloat16)
    return jnp.dot(a, wd_ref[...], preferred_element_type=jnp.float32)


def _rope(x, cos, sin_lo, sin_hi):
    return x * cos + pltpu.roll(x, 96, 1) * sin_lo + pltpu.roll(x, 32, 1) * sin_hi


def _ffn1_proj_kernel(x_ref, n1_ref, wg_ref, wu_ref, wd_ref, nm_ref, win_ref, invf_ref,
                      x1_ref, q_ref, k_ref, vt_ref, z_ref, cosr_scr, sinr_scr, *, tiles_per_seq, pos_offset):
    tr = x_ref.shape[0]

    @pl.when(pl.program_id(0) == 0)
    def _():
        r = lax.broadcasted_iota(jnp.int32, (tr, LANES), 0).astype(jnp.float32)
        cosr_scr[...] = jnp.cos(r * invf_ref[...])
        sinr_scr[...] = jnp.sin(r * invf_ref[...])

    x = x_ref[...]
    h = _rmsnorm(x, n1_ref[...]).astype(jnp.bfloat16)
    x1 = x + 0.5 * _swiglu(h, wg_ref, wu_ref, wd_ref)
    x1_ref[...] = x1

    hm = _rmsnorm(x1, nm_ref[...]).astype(jnp.bfloat16)
    proj = jnp.dot(hm, win_ref[...], preferred_element_type=jnp.float32)

    t0 =((pl.program_id(0) % tiles_per_seq) * tr + pos_offset).astype(jnp.float32)
    cos0 = jnp.cos(t0 * invf_ref[...])
    sin0 = jnp.sin(t0 * invf_ref[...])
    cos = cos0 * cosr_scr[...] - sin0 * sinr_scr[...]
    sin = sin0 * cosr_scr[...] + cos0 * sinr_scr[...]
    lane = lax.broadcasted_iota(jnp.int32, (tr, LANES), 1)
    first_half = (lane % HEAD_DIM) < (HEAD_DIM // 2)
    sin_lo = jnp.where(first_half, -sin, 0.0)
    sin_hi = jnp.where(first_half, 0.0, sin)
    scale = HEAD_DIM ** -0.5 * math.log2(math.e)
    for c in range(D_ATTN // LANES):
        sl = slice(c * LANES, (c + 1) * LANES)
        qc = proj[:, c * LANES:(c + 1) * LANES]
        kc = proj[:, D_ATTN + c * LANES:D_ATTN + (c + 1) * LANES]
        q_ref[:, sl] = (_rope(qc, cos, sin_lo, sin_hi) * scale).astype(jnp.bfloat16)
        k_ref[:, sl] = _rope(kc, cos, sin_lo, sin_hi).astype(jnp.bfloat16)
    hw = 2 * HEAD_DIM
    for hd in range(N_DIFF_HEADS):
        vh = proj[:, 2 * D_ATTN + hd * hw:2 * D_ATTN + (hd + 1) * hw]
        vt_ref[0, hd, 0] = vh.T.astype(jnp.bfloat16)
    ua = proj[:, 3 * D_ATTN:3 * D_ATTN + D_CONV]
    ug = proj[:, 3 * D_ATTN + D_CONV:]
    z_ref[...] = ua * jax.nn.sigmoid(ug)


def _resident(shape):
    return pl.BlockSpec(shape, lambda i: (0,) * len(shape), pipeline_mode=pl.Buffered(1))


def _ffn1_proj(x2d, n1, wg, wu, wd, nm, win, invf, *, row_tile, rows_per_seq, pos_offset):
    rows = x2d.shape[0]
    assert rows % row_tile == 0 and rows_per_seq % row_tile == 0
    tiles_per_seq = rows_per_seq // row_tile
    hw = 2 * HEAD_DIM
    row_spec = lambda w: pl.BlockSpec((row_tile, w), lambda i: (i, 0))
    vt_spec = pl.BlockSpec((1, N_DIFF_HEADS, 1, hw, row_tile),
                           lambda i: (i // tiles_per_seq, 0, i % tiles_per_seq, 0, 0))
    kern = functools.partial(_ffn1_proj_kernel, tiles_per_seq=tiles_per_seq, pos_offset=pos_offset)
    return pl.pallas_call(
        kern,
        grid=(rows // row_tile,),
        in_specs=[row_spec(D_MODEL), _resident((1, D_MODEL)),
                  _resident((D_MODEL, D_FF)), _resident((D_MODEL, D_FF)), _resident((D_FF, D_MODEL)),
                  _resident((1, D_MODEL)), _resident((D_MODEL, D_IN_PROJ)), _resident((1, LANES))],
        out_specs=[row_spec(D_MODEL), row_spec(D_ATTN), row_spec(D_ATTN), vt_spec, row_spec(D_CONV)],
        out_shape=[jax.ShapeDtypeStruct((rows, D_MODEL), jnp.float32),
                   jax.ShapeDtypeStruct((rows, D_ATTN), jnp.bfloat16),
                   jax.ShapeDtypeStruct((rows, D_ATTN), jnp.bfloat16),
                   jax.ShapeDtypeStruct((rows // rows_per_seq, N_DIFF_HEADS, tiles_per_seq, hw, row_tile),
                                        jnp.bfloat16),
                   jax.ShapeDtypeStruct((rows, D_CONV), jnp.float32)],
        scratch_shapes=[pltpu.VMEM((row_tile, LANES), jnp.float32),
                        pltpu.VMEM((row_tile, LANES), jnp.float32)],
        compiler_params=pltpu.CompilerParams(dimension_semantics=("arbitrary",),
                                             vmem_limit_bytes=VMEM_LIMIT_BYTES),
        name="ffn1_proj",
    )(x2d, n1, wg, wu, wd, nm, win, invf)


def _attn_kernel(q_ref, qn_ref, k_ref, vt_ref, km_ref, vmt_ref, lq1_ref, lk1_ref, lq2_ref, lk2_ref, sw_ref,
                 o_ref, m_scr, acc_scr, s0_scr, s1_scr, p0_scr, p1_scr, a0_scr, a1_scr, c0_scr, c1_scr):
    tq = q_ref.shape[1]
    hw, tk = vt_ref.shape[-2:]
    qi = pl.program_id(2)
    nt = (((1,), (1,)), ((), ()))
    n_chunks = 2 * tq // Q_CHUNK

    def stack_maps(q):
        lane = lax.broadcasted_iota(jnp.int32, q.shape, 1)
        zero = jnp.zeros_like(q)
        return jnp.concatenate([jnp.where(lane < HEAD_DIM, q, zero), jnp.where(lane >= HEAD_DIM, q, zero)], axis=0)

    qs = stack_maps(q_ref[0])

    def with_ones(vt):
        return jnp.concatenate([vt, jnp.ones((ONES_ROWS, vt.shape[1]), vt.dtype)], axis=0)

    s_bufs, p_bufs, a_bufs = ((s0_scr, c0_scr), (s1_scr, c1_scr)), (p0_scr, p1_scr), (a0_scr, a1_scr)

    def scores_into(bufs, j, q_stacked):
        s_scr, c_scr = bufs
        kb = k_ref[0, pl.ds(pl.multiple_of(j * tk, tk), tk), :]
        s = lax.dot_general(kb, q_stacked, nt, preferred_element_type=jnp.float32)
        s_scr[:, 0:2 * tq] = s
        c_scr[...] = jnp.max(s, axis=0, keepdims=True)

    def scores(j, par):
        scores_into(s_bufs[par], j, qs)

    def softmax(par, key_off=None, src=None):
        (s_scr, c_scr), p_scr, a_scr = (s_bufs[par] if src is None else src), p_bufs[par], a_bufs[par]
        for c in range(n_chunks):
            sl = slice(c * Q_CHUNK, (c + 1) * Q_CHUNK)
            rel = tk if key_off is None else (c * Q_CHUNK) % tq - key_off
            if rel <= -Q_CHUNK:
                p_scr[:, sl] = jnp.zeros((tk, Q_CHUNK), jnp.bfloat16)
                a_scr[:, sl] = jnp.ones((1, Q_CHUNK), jnp.float32)
                continue
            if rel >= tk:
                nk = tk
                load = lambda: s_scr[:, sl]
                col_max = c_scr[:, sl]
            else:
                nk = min(tk, rel + Q_CHUNK)
                if nk < tk:
                    p_scr[nk:tk, sl] = jnp.zeros((tk - nk, Q_CHUNK), jnp.bfloat16)

                def load(nk=nk, rel=rel):
                    s = s_scr[0:nk, sl]
                    key = lax.broadcasted_iota(jnp.int32, s.shape, 0)
                    qry = lax.broadcasted_iota(jnp.int32, s.shape, 1) + rel
                    return jnp.where(key <= qry, s, MASK_VALUE)
                col_max = jnp.max(load(), axis=0, keepdims=True)
            m_prev = m_scr[:, sl]
            m_new = jnp.maximum(m_prev, col_max)
            a_scr[:, sl] = jnp.exp2(m_prev - m_new)
            p_scr[0:nk, sl] = jnp.exp2(load() - m_new).astype(jnp.bfloat16)
            m_scr[:, sl] = m_new

    def values(j, par):
        vtb = with_ones(vt_ref[0, 0, j])
        acc_scr[...] = a_bufs[par][...] * acc_scr[...] + jnp.dot(vtb, p_bufs[par][...],
                                                                preferred_element_type=jnp.float32)

    def reset_max():
        m_scr[...] = jnp.full(m_scr.shape, MASK_VALUE, jnp.float32)

    def reset_acc():
        acc_scr[...] = jnp.zeros(acc_scr.shape, jnp.float32)

    def meta_softmax():
        s = lax.dot_general(km_ref[0:N_META, :], qs, nt, preferred_element_type=jnp.float32)
        m_prev = m_scr[...]
        m_new = jnp.maximum(m_prev, jnp.max(s, axis=0, keepdims=True))
        pad = jnp.zeros((vmt_ref.shape[-1] - N_META, 2 * tq), jnp.bfloat16)
        p_pad = jnp.concatenate([jnp.exp2(s - m_new).astype(jnp.bfloat16), pad], axis=0)
        return jnp.exp2(m_prev - m_new), p_pad

    def meta_values(alpha, p_pad):
        acc_scr[...] = alpha * acc_scr[...] + jnp.dot(with_ones(vmt_ref[0, 0, 0]), p_pad,
                                                      preferred_element_type=jnp.float32)

    @pl.when(qi == 0)
    def _():
        scores(0, 0)
        scores(1, 1)
        reset_max()
        reset_acc()
        softmax(0, key_off=0)

    def pair(p, carry):
        t = 2 * p + 1
        scores(t + 1, 0)
        softmax(1)
        values(t - 1, 0)
        scores(t + 2, 1)
        softmax(0)
        values(t, 1)
        return carry

    lax.fori_loop(0, jnp.maximum(qi - 1, 0), pair, 0)

    @pl.when(qi > 0)
    def _():
        scores(2 * qi, 0)
        softmax(1)
        values(2 * qi - 2, 0)
        scores(2 * qi + 1, 1)
        softmax(0, key_off=0)
        values(2 * qi - 1, 1)

    qs_next = stack_maps(qn_ref[0])
    scores_into(s_bufs[0], 0, qs_next)
    softmax(1, key_off=tk)
    meta_alpha, meta_p = meta_softmax()
    values(2 * qi, 0)
    values(2 * qi + 1, 1)
    meta_values(meta_alpha, meta_p)

    lam = (jnp.exp(jnp.sum(lq1_ref[...] * lk1_ref[...], axis=-1, keepdims=True))
           - jnp.exp(jnp.sum(lq2_ref[...] * lk2_ref[...], axis=-1, keepdims=True)) + LAMBDA_INIT)
    o = acc_scr[0:hw, :] / acc_scr[hw:hw + 1, :]
    o = o[:, :tq] - lam * o[:, tq:]
    o = o * lax.rsqrt(jnp.mean(o * o, axis=0, keepdims=True) + NORM_EPS)
    o_ref[0] = (o.T * sw_ref[...] * (1.0 - LAMBDA_INIT)).astype(o_ref.dtype)

    reset_max()
    reset_acc()
    scores_into(s_bufs[1], 1, qs_next)
    softmax(0)


def _diff_attention(q, k, vt, km, vmt, lq1, lk1, lq2, lk2, sw):
    b, t, _ = q.shape
    n_kv, hw, tk = vt.shape[2:]
    tq = 2 * tk
    assert t == n_kv * tk and t % tq == 0 and tk % Q_CHUNK == 0
    small = lambda shape: pl.BlockSpec(shape, lambda bi, hi, qi: (0,) * len(shape))
    n_q = t // tq
    return pl.pallas_call(
        _attn_kernel,
        grid=(b, N_DIFF_HEADS, n_q),
        in_specs=[pl.BlockSpec((1, tq, hw), lambda bi, hi, qi: (bi, qi, hi)),
                  pl.BlockSpec((1, tq, hw), lambda bi, hi, qi: (bi, jnp.minimum(qi + 1, n_q - 1), hi)),
                  pl.BlockSpec((1, t, hw), lambda bi, hi, qi: (bi, 0, hi)),
                  pl.BlockSpec((1, 1, n_kv, hw, tk), lambda bi, hi, qi: (bi, hi, 0, 0, 0)),
                  pl.BlockSpec((km.shape[0], hw), lambda bi, hi, qi: (0, hi)),
                  pl.BlockSpec((1, 1, 1) + vmt.shape[3:], lambda bi, hi, qi: (0, hi, 0, 0, 0)),
                  small((1, HEAD_DIM)), small((1, HEAD_DIM)), small((1, HEAD_DIM)), small((1, HEAD_DIM)),
                  small((1, hw))],
        out_specs=pl.BlockSpec((1, tq, hw), lambda bi, hi, qi: (bi, qi, hi)),
        out_shape=jax.ShapeDtypeStruct((b, t, D_ATTN), jnp.bfloat16),
        scratch_shapes=[pltpu.VMEM((1, 2 * tq), jnp.float32),
                        pltpu.VMEM((hw + ONES_ROWS, 2 * tq), jnp.float32),
                        pltpu.VMEM((tk, 2 * tq + SCORE_PAD), jnp.float32),
                        pltpu.VMEM((tk, 2 * tq + SCORE_PAD), jnp.float32),
                        pltpu.VMEM((tk, 2 * tq), jnp.bfloat16),
                        pltpu.VMEM((tk, 2 * tq), jnp.bfloat16),
                        pltpu.VMEM((1, 2 * tq), jnp.float32),
                        pltpu.VMEM((1, 2 * tq), jnp.float32),
                        pltpu.VMEM((1, 2 * tq), jnp.float32),
                        pltpu.VMEM((1, 2 * tq), jnp.float32)],
        compiler_params=pltpu.CompilerParams(dimension_semantics=("arbitrary", "arbitrary", "arbitrary"),
                                             vmem_limit_bytes=VMEM_LIMIT_BYTES),
        name="diff_attn",
    )(q, q, k, vt, km, vmt, lq1, lk1, lq2, lk2, sw)


def _mix_ffn2_kernel(x1_ref, a_ref, z0_ref, znext_ref, zprev_ref, zstart_ref, zero_ref, cw_ref, cb_ref, lg_ref, lb_ref,
                     wo_ref, n2_ref, wg_ref, wu_ref, wd_ref, nf_ref, y_ref, zw_scr, c_scr, *, tiles_per_seq):
    tr = x1_ref.shape[0]
    i = pl.program_id(0)

    def fill_window(z_tile_ref, halo):
        zw_scr[0:HALO, :] = halo
        zw_scr[HALO:HALO + tr, :] = z_tile_ref[...]
        zw_scr[HALO + tr:, :] = jnp.zeros((SUBLANES, D_CONV), jnp.float32)

    def conv_rows(r0):
        base = HALO - (CONV_WIDTH - 1)
        conv = None
        for rho in range(SUBLANES):
            group = None
            for o in range(rho, HALO + 1, SUBLANES):
                if o < base:
                    continue
                lo = r0 + o - rho
                term = cw_ref[o - base:o - base + 1, :] * zw_scr[lo:lo + CONV_ROWS + SUBLANES, :]
                group = term if group is None else group + term
            shifted = group[rho:rho + CONV_ROWS]
            conv = shifted if conv is None else conv + shifted
        conv = conv + cb_ref[...]
        mu = jnp.mean(conv, axis=-1, keepdims=True)
        cc = conv - mu
        var = jnp.mean(cc * cc, axis=-1, keepdims=True)
        c = cc * lax.rsqrt(var + NORM_EPS) * lg_ref[...] + lb_ref[...]
        c = c * jax.nn.sigmoid(c)
        c_scr[r0:r0 + CONV_ROWS, :] = c.astype(jnp.bfloat16)
        bits = pltpu.bitcast(c[0:SUBLANES, 0:FF_CHUNK], jnp.int32) & zero_ref[...]
        return pltpu.bitcast(bits, jnp.float32)

    conv_starts = list(range(0, tr, CONV_ROWS))

    @pl.when(i == 0)
    def _():
        fill_window(z0_ref, zstart_ref[...])
        for r0 in conv_starts:
            conv_rows(r0)

    x2 = (x1_ref[...]
          + jnp.dot(a_ref[...], wo_ref[0:D_ATTN, :], preferred_element_type=jnp.float32)
          + jnp.dot(c_scr[...], wo_ref[D_ATTN:, :], preferred_element_type=jnp.float32))
    h = _rmsnorm(x2, n2_ref[...]).astype(jnp.bfloat16)

    next_starts_seq = ((i + 1) % tiles_per_seq) == 0
    fill_window(znext_ref, jnp.where(next_starts_seq, zstart_ref[...], zprev_ref[...]))
    ffn = None
    tokens = {}
    for ci, c0 in enumerate(range(0, D_FF, FF_CHUNK)):
        cols = slice(c0, c0 + FF_CHUNK)
        g = jnp.dot(h, wg_ref[:, cols], preferred_element_type=jnp.float32)
        u = jnp.dot(h, wu_ref[:, cols], preferred_element_type=jnp.float32)
        if ci in tokens:
            g = jnp.concatenate([g[0:SUBLANES] + tokens[ci], g[SUBLANES:]], axis=0)
        act = (g * jax.nn.sigmoid(g) * u).astype(jnp.bfloat16)
        d = jnp.dot(act, wd_ref[cols, :], preferred_element_type=jnp.float32)
        ffn = d if ffn is None else ffn + d
        if ci < len(conv_starts):
            tokens[ci + TOKEN_LAG] = conv_rows(conv_starts[ci])
    x3 = x2 + 0.5 * ffn
    y_ref[...] = _rmsnorm(x3, nf_ref[...])


def _mix_ffn2(x1, a, z, zstart, cw, cb, lg, lb, wo, n2, wg, wu, wd, nf, *, rows_per_seq):
    rows = x1.shape[0]
    tr = ROW_TILE
    assert rows % tr == 0 and rows_per_seq % tr == 0 and tr % HALO == 0
    assert D_FF % FF_CHUNK == 0 and tr % CONV_ROWS == 0 and D_FF // FF_CHUNK >= tr // CONV_ROWS + TOKEN_LAG
    n_tiles = rows // tr
    row_spec = lambda w: pl.BlockSpec((tr, w), lambda i: (i, 0))
    z0_spec = pl.BlockSpec((tr, D_CONV), lambda i: (0, 0), pipeline_mode=pl.Buffered(1))
    znext_spec = pl.BlockSpec((tr, D_CONV), lambda i: (jnp.minimum(i + 1, n_tiles - 1), 0))
    halo_spec = pl.BlockSpec((HALO, D_CONV), lambda i: ((i + 1) * (tr // HALO) - 1, 0))
    kern = functools.partial(_mix_ffn2_kernel, tiles_per_seq=rows_per_seq // tr)
    return pl.pallas_call(
        kern,
        grid=(n_tiles,),
        in_specs=[row_spec(D_MODEL), row_spec(D_ATTN), z0_spec, znext_spec, halo_spec,
                  _resident((HALO, D_CONV)), _resident((SUBLANES, FF_CHUNK)),
                  _resident((CONV_WIDTH, D_CONV)), _resident((1, D_CONV)),
                  _resident((1, D_CONV)), _resident((1, D_CONV)),
                  _resident((D_ATTN + D_CONV, D_MODEL)), _resident((1, D_MODEL)),
                  _resident((D_MODEL, D_FF)), _resident((D_MODEL, D_FF)), _resident((D_FF, D_MODEL)),
                  _resident((1, D_MODEL))],
        out_specs=row_spec(D_MODEL),
        out_shape=jax.ShapeDtypeStruct((rows, D_MODEL), jnp.float32),
        scratch_shapes=[pltpu.VMEM((HALO + tr + SUBLANES, D_CONV), jnp.float32),
                        pltpu.VMEM((tr, D_CONV), jnp.bfloat16)],
        compiler_params=pltpu.CompilerParams(dimension_semantics=("arbitrary",),
                                             vmem_limit_bytes=VMEM_LIMIT_BYTES),
        name="mix_ffn2",
    )(x1, a, z, z, z, zstart, jnp.zeros((SUBLANES, FF_CHUNK), jnp.int32), cw, cb, lg, lb, wo, n2, wg, wu, wd, nf)


def kernel(x, meta_tokens, ffn1_norm, ffn1_w_gate, ffn1_w_up, ffn1_w_down, mix_norm, w_in, lambda_q1, lambda_k1, lambda_q2, lambda_k2, subln_w, conv_w, conv_b, conv_ln_g, conv_ln_b, w_out, ffn2_norm, ffn2_w_gate, ffn2_w_up, ffn2_w_down, final_norm):
    b, t, d = x.shape
    bf16 = jnp.bfloat16
    row = lambda v: v.reshape(1, -1)

    inv_freq = ROPE_THETA ** (-jnp.arange(0, HEAD_DIM, 2, dtype=jnp.float32) / HEAD_DIM)
    invf = jnp.tile(inv_freq, LANES // (HEAD_DIM // 2)).reshape(1, LANES)

    ffn1_args = (row(ffn1_norm[0]), ffn1_w_gate[0].astype(bf16), ffn1_w_up[0].astype(bf16),
                 ffn1_w_down[0].astype(bf16), row(mix_norm[0]), w_in[0].astype(bf16), invf)
    x1, q, k, vt, z = _ffn1_proj(x.reshape(b * t, d), *ffn1_args,
                                 row_tile=ROW_TILE, rows_per_seq=t, pos_offset=N_META)
    meta = jnp.concatenate([meta_tokens, jnp.zeros((META_TILE - N_META, d), meta_tokens.dtype)], axis=0)
    _, _, km, vmt, zm = _ffn1_proj(meta, *ffn1_args, row_tile=META_TILE, rows_per_seq=META_TILE, pos_offset=0)

    a = _diff_attention(q.reshape(b, t, D_ATTN), k.reshape(b, t, D_ATTN), vt, km, vmt,
                        row(lambda_q1[0]), row(lambda_k1[0]), row(lambda_q2[0]), row(lambda_k2[0]),
                        row(subln_w[0]))

    zstart = jnp.concatenate([jnp.zeros((HALO - N_META, D_CONV), jnp.float32), zm[:N_META]], axis=0)
    y = _mix_ffn2(x1, a.reshape(b * t, D_ATTN), z, zstart, conv_w[0], row(conv_b[0]),
                  row(conv_ln_g[0]), row(conv_ln_b[0]), w_out[0].astype(bf16), row(ffn2_norm[0]),
                  ffn2_w_gate[0].astype(bf16), ffn2_w_up[0].astype(bf16), ffn2_w_down[0].astype(bf16),
                  row(final_norm), rows_per_seq=t)
    return y.reshape(b, t, d)
```

```python
import functools
import math

import jax
import jax.numpy as jnp
from jax import lax
from jax.experimental import pallas as pl
from jax.experimental.pallas import tpu as pltpu

D_MODEL = 1024
N_META = 16
D_ATTN = 512
D_CONV = 512
HEAD_DIM = 64
N_DIFF_HEADS = 4
CONV_WIDTH = 31
D_FF = 2816
ROPE_THETA = 10000.0
NORM_EPS = 1e-5
D_IN_PROJ = 3 * D_ATTN + 2 * D_CONV
LAMBDA_INIT = 0.8 - 0.6 * math.exp(-0.3 * 0)

LANES = 128
SUBLANES = 8
HALO = 32
ROW_TILE = 512
META_TILE = 128
Q_CHUNK = 256
FF_CHUNK = 256
CONV_ROWS = 64
SCORE_PAD = 128
TOKEN_LAG = 2
ONES_ROWS = 16
VMEM_LIMIT_BYTES = 56 * 1024 * 1024
MASK_VALUE = -1e30


def _rmsnorm(x, g):
    return x * lax.rsqrt(jnp.mean(x * x, axis=-1, keepdims=True) + NORM_EPS) * g


def _swiglu(h_bf16, wg_ref, wu_ref, wd_ref):
    g = jnp.dot(h_bf16, wg_ref[...], preferred_element_type=jnp.float32)
    u = jnp.dot(h_bf16, wu_ref[...], preferred_element_type=jnp.float32)
    a = (g * jax.nn.sigmoid(g) * u).astype(jnp.bfloat16)
    return jnp.dot(a, wd_ref[...], preferred_element_type=jnp.float32)


def _rope(x, cos, sin_lo, sin_hi):
    return x * cos + pltpu.roll(x, 96, 1) * sin_lo + pltpu.roll(x, 32, 1) * sin_hi


def _ffn1_proj_kernel(x_ref, n1_ref, wg_ref, wu_ref, wd_ref, nm_ref, win_ref, invf_ref,
                      x1_ref, q_ref, k_ref, vt_ref, z_ref, cosr_scr, sinr_scr, *, tiles_per_seq, pos_offset):
    tr = x_ref.shape[0]

    @pl.when(pl.program_id(0) == 0)
    def _():
        r = lax.broadcasted_iota(jnp.int32, (tr, LANES), 0).astype(jnp.float32)
        cosr_scr[...] = jnp.cos(r * invf_ref[...])
        sinr_scr[...] = jnp.sin(r * invf_ref[...])

    x = x_ref[...]
    h = _rmsnorm(x, n1_ref[...]).astype(jnp.bfloat16)
    x1 = x + 0.5 * _swiglu(h, wg_ref, wu_ref, wd_ref)
    x1_ref[...] = x1

    hm = _rmsnorm(x1, nm_ref[...]).astype(jnp.bfloat16)
    proj = jnp.dot(hm, win_ref[...], preferred_element_type=jnp.float32)

    t0 =((pl.program_id(0) % tiles_per_seq) * tr + pos_offset).astype(jnp.float32)
    cos0 = jnp.cos(t0 * invf_ref[...])
    sin0 = jnp.sin(t0 * invf_ref[...])
    cos = cos0 * cosr_scr[...] - sin0 * sinr_scr[...]
    sin = sin0 * cosr_scr[...] + cos0 * sinr_scr[...]
    lane = lax.broadcasted_iota(jnp.int32, (tr, LANES), 1)
    first_half = (lane % HEAD_DIM) < (HEAD_DIM // 2)
    sin_lo = jnp.where(first_half, -sin, 0.0)
    sin_hi = jnp.where(first_half, 0.0, sin)
    scale = HEAD_DIM ** -0.5 * math.log2(math.e)
    for c in range(D_ATTN // LANES):
        sl = slice(c * LANES, (c + 1) * LANES)
        qc = proj[:, c * LANES:(c + 1) * LANES]
        kc = proj[:, D_ATTN + c * LANES:D_ATTN + (c + 1) * LANES]
        q_ref[:, sl] = (_rope(qc, cos, sin_lo, sin_hi) * scale).astype(jnp.bfloat16)
        k_ref[:, sl] = _rope(kc, cos, sin_lo, sin_hi).astype(jnp.bfloat16)
    hw = 2 * HEAD_DIM
    for hd in range(N_DIFF_HEADS):
        vh = proj[:, 2 * D_ATTN + hd * hw:2 * D_ATTN + (hd + 1) * hw]
        vt_ref[0, hd, 0] = vh.T.astype(jnp.bfloat16)
    ua = proj[:, 3 * D_ATTN:3 * D_ATTN + D_CONV]
    ug = proj[:, 3 * D_ATTN + D_CONV:]
    z_ref[...] = ua * jax.nn.sigmoid(ug)


def _resident(shape):
    return pl.BlockSpec(shape, lambda i: (0,) * len(shape), pipeline_mode=pl.Buffered(1))


def _ffn1_proj(x2d, n1, wg, wu, wd, nm, win, invf, *, row_tile, rows_per_seq, pos_offset):
    rows = x2d.shape[0]
    assert rows % row_tile == 0 and rows_per_seq % row_tile == 0
    tiles_per_seq = rows_per_seq // row_tile
    hw = 2 * HEAD_DIM
    row_spec = lambda w: pl.BlockSpec((row_tile, w), lambda i: (i, 0))
    vt_spec = pl.BlockSpec((1, N_DIFF_HEADS, 1, hw, row_tile),
                           lambda i: (i // tiles_per_seq, 0, i % tiles_per_seq, 0, 0))
    kern = functools.partial(_ffn1_proj_kernel, tiles_per_seq=tiles_per_seq, pos_offset=pos_offset)
    return pl.pallas_call(
        kern,
        grid=(rows // row_tile,),
        in_specs=[row_spec(D_MODEL), _resident((1, D_MODEL)),
                  _resident((D_MODEL, D_FF)), _resident((D_MODEL, D_FF)), _resident((D_FF, D_MODEL)),
                  _resident((1, D_MODEL)), _resident((D_MODEL, D_IN_PROJ)), _resident((1, LANES))],
        out_specs=[row_spec(D_MODEL), row_spec(D_ATTN), row_spec(D_ATTN), vt_spec, row_spec(D_CONV)],
        out_shape=[jax.ShapeDtypeStruct((rows, D_MODEL), jnp.float32),
                   jax.ShapeDtypeStruct((rows, D_ATTN), jnp.bfloat16),
                   jax.ShapeDtypeStruct((rows, D_ATTN), jnp.bfloat16),
                   jax.ShapeDtypeStruct((rows // rows_per_seq, N_DIFF_HEADS, tiles_per_seq, hw, row_tile),
                                        jnp.bfloat16),
                   jax.ShapeDtypeStruct((rows, D_CONV), jnp.float32)],
        scratch_shapes=[pltpu.VMEM((row_tile, LANES), jnp.float32),
                        pltpu.VMEM((row_tile, LANES), jnp.float32)],
        compiler_params=pltpu.CompilerParams(dimension_semantics=("arbitrary",),
                                             vmem_limit_bytes=VMEM_LIMIT_BYTES),
        name="ffn1_proj",
    )(x2d, n1, wg, wu, wd, nm, win, invf)


def _attn_kernel(q_ref, qn_ref, k_ref, vt_ref, km_ref, vmt_ref, lq1_ref, lk1_ref, lq2_ref, lk2_ref, sw_ref,
                 o_ref, m_scr, acc_scr, s0_scr, s1_scr, p0_scr, p1_scr, a0_scr, a1_scr, c0_scr, c1_scr):
    tq = q_ref.shape[1]
    hw, tk = vt_ref.shape[-2:]
    qi = pl.program_id(2)
    nt = (((1,), (1,)), ((), ()))
    n_chunks = 2 * tq // Q_CHUNK

    def stack_maps(q):
        lane = lax.broadcasted_iota(jnp.int32, q.shape, 1)
        zero = jnp.zeros_like(q)
        return jnp.concatenate([jnp.where(lane < HEAD_DIM, q, zero), jnp.where(lane >= HEAD_DIM, q, zero)], axis=0)

    qs = stack_maps(q_ref[0])

    def with_ones(vt):
        return jnp.concatenate([vt, jnp.ones((ONES_ROWS, vt.shape[1]), vt.dtype)], axis=0)

    s_bufs, p_bufs, a_bufs = ((s0_scr, c0_scr), (s1_scr, c1_scr)), (p0_scr, p1_scr), (a0_scr, a1_scr)

    all_lanes = [(0, 2 * tq)]
    late_lanes = [(tk, tq), (tq + tk, 2 * tq)]

    def scores_into(bufs, j, q_stacked, lanes=all_lanes):
        s_scr, c_scr = bufs
        kb = k_ref[0, pl.ds(pl.multiple_of(j * tk, tk), tk), :]
        for lo, hi in lanes:
            s = lax.dot_general(kb, q_stacked[lo:hi], nt, preferred_element_type=jnp.float32)
            s_scr[:, lo:hi] = s
            c_scr[:, lo:hi] = jnp.max(s, axis=0, keepdims=True)

    def scores(j, par, lanes=all_lanes):
        scores_into(s_bufs[par], j, qs, lanes)

    def softmax(par, key_off=None, src=None):
        (s_scr, c_scr), p_scr, a_scr = (s_bufs[par] if src is None else src), p_bufs[par], a_bufs[par]
        for c in range(n_chunks):
            sl = slice(c * Q_CHUNK, (c + 1) * Q_CHUNK)
            rel = tk if key_off is None else (c * Q_CHUNK) % tq - key_off
            if rel <= -Q_CHUNK:
                continue
            if rel >= tk:
                nk = tk
                load = lambda: s_scr[:, sl]
                col_max = c_scr[:, sl]
            else:
                nk = min(tk, rel + Q_CHUNK)
                if nk < tk:
                    p_scr[nk:tk, sl] = jnp.zeros((tk - nk, Q_CHUNK), jnp.bfloat16)

                def load(nk=nk, rel=rel):
                    s = s_scr[0:nk, sl]
                    key = lax.broadcasted_iota(jnp.int32, s.shape, 0)
                    qry = lax.broadcasted_iota(jnp.int32, s.shape, 1) + rel
                    return jnp.where(key <= qry, s, MASK_VALUE)
                col_max = jnp.max(load(), axis=0, keepdims=True)
            m_prev = m_scr[:, sl]
            m_new = jnp.maximum(m_prev, col_max)
            a_scr[:, sl] = jnp.exp2(m_prev - m_new)
            p_scr[0:nk, sl] = jnp.exp2(load() - m_new).astype(jnp.bfloat16)
            m_scr[:, sl] = m_new

    def values(j, par, lanes=all_lanes):
        vtb = with_ones(vt_ref[0, 0, j])
        for lo, hi in lanes:
            acc_scr[:, lo:hi] = (a_bufs[par][:, lo:hi] * acc_scr[:, lo:hi]
                                 + jnp.dot(vtb, p_bufs[par][:, lo:hi], preferred_element_type=jnp.float32))

    def reset_max():
        m_scr[...] = jnp.full(m_scr.shape, MASK_VALUE, jnp.float32)

    def reset_acc():
        acc_scr[...] = jnp.zeros(acc_scr.shape, jnp.float32)

    def meta_softmax():
        s = lax.dot_general(km_ref[0:N_META, :], qs, nt, preferred_element_type=jnp.float32)
        m_prev = m_scr[...]
        m_new = jnp.maximum(m_prev, jnp.max(s, axis=0, keepdims=True))
        pad = jnp.zeros((vmt_ref.shape[-1] - N_META, 2 * tq), jnp.bfloat16)
        p_pad = jnp.concatenate([jnp.exp2(s - m_new).astype(jnp.bfloat16), pad], axis=0)
        return jnp.exp2(m_prev - m_new), p_pad

    def meta_values(alpha, p_pad):
        acc_scr[...] = alpha * acc_scr[...] + jnp.dot(with_ones(vmt_ref[0, 0, 0]), p_pad,
                                                      preferred_element_type=jnp.float32)

    @pl.when(qi == 0)
    def _():
        scores(0, 0)
        scores(1, 1, late_lanes)
        reset_max()
        reset_acc()
        softmax(0, key_off=0)

    def pair(p, carry):
        t = 2 * p + 1
        scores(t + 1, 0)
        softmax(1)
        values(t - 1, 0)
        scores(t + 2, 1)
        softmax(0)
        values(t, 1)
        return carry

    lax.fori_loop(0, jnp.maximum(qi - 1, 0), pair, 0)

    @pl.when(qi > 0)
    def _():
        scores(2 * qi, 0)
        softmax(1)
        values(2 * qi - 2, 0)
        scores(2 * qi + 1, 1, late_lanes)
        softmax(0, key_off=0)
        values(2 * qi - 1, 1)

    qs_next = stack_maps(qn_ref[0])
    scores_into(s_bufs[0], 0, qs_next)
    softmax(1, key_off=tk)
    meta_alpha, meta_p = meta_softmax()
    values(2 * qi, 0)
    values(2 * qi + 1, 1, late_lanes)
    meta_values(meta_alpha, meta_p)

    lam = (jnp.exp(jnp.sum(lq1_ref[...] * lk1_ref[...], axis=-1, keepdims=True))
           - jnp.exp(jnp.sum(lq2_ref[...] * lk2_ref[...], axis=-1, keepdims=True)) + LAMBDA_INIT)
    o = acc_scr[0:hw, :] / acc_scr[hw:hw + 1, :]
    o = o[:, :tq] - lam * o[:, tq:]
    o = o * lax.rsqrt(jnp.mean(o * o, axis=0, keepdims=True) + NORM_EPS)
    o_ref[0] = (o.T * sw_ref[...] * (1.0 - LAMBDA_INIT)).astype(o_ref.dtype)

    reset_max()
    reset_acc()
    scores_into(s_bufs[1], 1, qs_next)
    softmax(0)


def _diff_attention(q, k, vt, km, vmt, lq1, lk1, lq2, lk2, sw):
    b, t, _ = q.shape
    n_kv, hw, tk = vt.shape[2:]
    tq = 2 * tk
    assert t == n_kv * tk and t % tq == 0 and tk % Q_CHUNK == 0
    small = lambda shape: pl.BlockSpec(shape, lambda bi, hi, qi: (0,) * len(shape))
    n_q = t // tq
    return pl.pallas_call(
        _attn_kernel,
        grid=(b, N_DIFF_HEADS, n_q),
        in_specs=[pl.BlockSpec((1, tq, hw), lambda bi, hi, qi: (bi, qi, hi)),
                  pl.BlockSpec((1, tq, hw), lambda bi, hi, qi: (bi, jnp.minimum(qi + 1, n_q - 1), hi)),
                  pl.BlockSpec((1, t, hw), lambda bi, hi, qi: (bi, 0, hi)),
                  pl.BlockSpec((1, 1, n_kv, hw, tk), lambda bi, hi, qi: (bi, hi, 0, 0, 0)),
                  pl.BlockSpec((km.shape[0], hw), lambda bi, hi, qi: (0, hi)),
                  pl.BlockSpec((1, 1, 1) + vmt.shape[3:], lambda bi, hi, qi: (0, hi, 0, 0, 0)),
                  small((1, HEAD_DIM)), small((1, HEAD_DIM)), small((1, HEAD_DIM)), small((1, HEAD_DIM)),
                  small((1, hw))],
        out_specs=pl.BlockSpec((1, tq, hw), lambda bi, hi, qi: (bi, qi, hi)),
        out_shape=jax.ShapeDtypeStruct((b, t, D_ATTN), jnp.bfloat16),
        scratch_shapes=[pltpu.VMEM((1, 2 * tq), jnp.float32),
                        pltpu.VMEM((hw + ONES_ROWS, 2 * tq), jnp.float32),
                        pltpu.VMEM((tk, 2 * tq + SCORE_PAD), jnp.float32),
                        pltpu.VMEM((tk, 2 * tq + SCORE_PAD), jnp.float32),
                        pltpu.VMEM((tk, 2 * tq), jnp.bfloat16),
                        pltpu.VMEM((tk, 2 * tq), jnp.bfloat16),
                        pltpu.VMEM((1, 2 * tq), jnp.float32),
                        pltpu.VMEM((1, 2 * tq), jnp.float32),
                        pltpu.VMEM((1, 2 * tq), jnp.float32),
                        pltpu.VMEM((1, 2 * tq), jnp.float32)],
        compiler_params=pltpu.CompilerParams(dimension_semantics=("arbitrary", "arbitrary", "arbitrary"),
                                             vmem_limit_bytes=VMEM_LIMIT_BYTES),
        name="diff_attn",
    )(q, q, k, vt, km, vmt, lq1, lk1, lq2, lk2, sw)


def _mix_ffn2_kernel(x1_ref, a_ref, z0_ref, znext_ref, zprev_ref, zstart_ref, zero_ref, cw_ref, cb_ref, lg_ref, lb_ref,
                     wo_ref, n2_ref, wg_ref, wu_ref, wd_ref, nf_ref, y_ref, zw_scr, c_scr, *, tiles_per_seq):
    tr = x1_ref.shape[0]
    i = pl.program_id(0)

    def fill_window(z_tile_ref, halo):
        zw_scr[0:HALO, :] = halo
        zw_scr[HALO:HALO + tr, :] = z_tile_ref[...]
        zw_scr[HALO + tr:, :] = jnp.zeros((SUBLANES, D_CONV), jnp.float32)

    def conv_rows(r0):
        base = HALO - (CONV_WIDTH - 1)
        conv = None
        for rho in range(SUBLANES):
            group = None
            for o in range(rho, HALO + 1, SUBLANES):
                if o < base:
                    continue
                lo = r0 + o - rho
                term = cw_ref[o - base:o - base + 1, :] * zw_scr[lo:lo + CONV_ROWS + SUBLANES, :]
                group = term if group is None else group + term
            shifted = group[rho:rho + CONV_ROWS]
            conv = shifted if conv is None else conv + shifted
        conv = conv + cb_ref[...]
        mu = jnp.mean(conv, axis=-1, keepdims=True)
        cc = conv - mu
        var = jnp.mean(cc * cc, axis=-1, keepdims=True)
        c = cc * lax.rsqrt(var + NORM_EPS) * lg_ref[...] + lb_ref[...]
        c = c * jax.nn.sigmoid(c)
        c_scr[r0:r0 + CONV_ROWS, :] = c.astype(jnp.bfloat16)
        bits = pltpu.bitcast(c[0:SUBLANES, 0:FF_CHUNK], jnp.int32) & zero_ref[...]
        return pltpu.bitcast(bits, jnp.float32)

    conv_starts = list(range(0, tr, CONV_ROWS))

    @pl.when(i == 0)
    def _():
        fill_window(z0_ref, zstart_ref[...])
        for r0 in conv_starts:
            conv_rows(r0)

    x2 = (x1_ref[...]
          + jnp.dot(a_ref[...], wo_ref[0:D_ATTN, :], preferred_element_type=jnp.float32)
          + jnp.dot(c_scr[...], wo_ref[D_ATTN:, :], preferred_element_type=jnp.float32))
    h = _rmsnorm(x2, n2_ref[...]).astype(jnp.bfloat16)

    next_starts_seq = ((i + 1) % tiles_per_seq) == 0
    fill_window(znext_ref, jnp.where(next_starts_seq, zstart_ref[...], zprev_ref[...]))
    ffn = None
    tokens = {}
    for ci, c0 in enumerate(range(0, D_FF, FF_CHUNK)):
        cols = slice(c0, c0 + FF_CHUNK)
        g = jnp.dot(h, wg_ref[:, cols], preferred_element_type=jnp.float32)
        u = jnp.dot(h, wu_ref[:, cols], preferred_element_type=jnp.float32)
        if ci in tokens:
            g = jnp.concatenate([g[0:SUBLANES] + tokens[ci], g[SUBLANES:]], axis=0)
        act = (g * jax.nn.sigmoid(g) * u).astype(jnp.bfloat16)
        d = jnp.dot(act, wd_ref[cols, :], preferred_element_type=jnp.float32)
        ffn = d if ffn is None else ffn + d
        if ci < len(conv_starts):
            tokens[ci + TOKEN_LAG] = conv_rows(conv_starts[ci])
    x3 = x2 + 0.5 * ffn
    y_ref[...] = _rmsnorm(x3, nf_ref[...])


def _mix_ffn2(x1, a, z, zstart, cw, cb, lg, lb, wo, n2, wg, wu, wd, nf, *, rows_per_seq):
    rows = x1.shape[0]
    tr = ROW_TILE
    assert rows % tr == 0 and rows_per_seq % tr == 0 and tr % HALO == 0
    assert D_FF % FF_CHUNK == 0 and tr % CONV_ROWS == 0 and D_FF // FF_CHUNK >= tr // CONV_ROWS + TOKEN_LAG
    n_tiles = rows // tr
    row_spec = lambda w: pl.BlockSpec((tr, w), lambda i: (i, 0))
    z0_spec = pl.BlockSpec((tr, D_CONV), lambda i: (0, 0), pipeline_mode=pl.Buffered(1))
    znext_spec = pl.BlockSpec((tr, D_CONV), lambda i: (jnp.minimum(i + 1, n_tiles - 1), 0))
    halo_spec = pl.BlockSpec((HALO, D_CONV), lambda i: ((i + 1) * (tr // HALO) - 1, 0))
    kern = functools.partial(_mix_ffn2_kernel, tiles_per_seq=rows_per_seq // tr)
    return pl.pallas_call(
        kern,
        grid=(n_tiles,),
        in_specs=[row_spec(D_MODEL), row_spec(D_ATTN), z0_spec, znext_spec, halo_spec,
                  _resident((HALO, D_CONV)), _resident((SUBLANES, FF_CHUNK)),
                  _resident((CONV_WIDTH, D_CONV)), _resident((1, D_CONV)),
                  _resident((1, D_CONV)), _resident((1, D_CONV)),
                  _resident((D_ATTN + D_CONV, D_MODEL)), _resident((1, D_MODEL)),
                  _resident((D_MODEL, D_FF)), _resident((D_MODEL, D_FF)), _resident((D_FF, D_MODEL)),
                  _resident((1, D_MODEL))],
        out_specs=row_spec(D_MODEL),
        out_shape=jax.ShapeDtypeStruct((rows, D_MODEL), jnp.float32),
        scratch_shapes=[pltpu.VMEM((HALO + tr + SUBLANES, D_CONV), jnp.float32),
                        pltpu.VMEM((tr, D_CONV), jnp.bfloat16)],
        compiler_params=pltpu.CompilerParams(dimension_semantics=("arbitrary",),
                                             vmem_limit_bytes=VMEM_LIMIT_BYTES),
        name="mix_ffn2",
    )(x1, a, z, z, z, zstart, jnp.zeros((SUBLANES, FF_CHUNK), jnp.int32), cw, cb, lg, lb, wo, n2, wg, wu, wd, nf)


def kernel(x, meta_tokens, ffn1_norm, ffn1_w_gate, ffn1_w_up, ffn1_w_down, mix_norm, w_in, lambda_q1, lambda_k1, lambda_q2, lambda_k2, subln_w, conv_w, conv_b, conv_ln_g, conv_ln_b, w_out, ffn2_norm, ffn2_w_gate, ffn2_w_up, ffn2_w_down, final_norm):
    b, t, d = x.shape
    bf16 = jnp.bfloat16
    row = lambda v: v.reshape(1, -1)

    inv_freq = ROPE_THETA ** (-jnp.arange(0, HEAD_DIM, 2, dtype=jnp.float32) / HEAD_DIM)
    invf = jnp.tile(inv_freq, LANES // (HEAD_DIM // 2)).reshape(1, LANES)

    ffn1_args = (row(ffn1_norm[0]), ffn1_w_gate[0].astype(bf16), ffn1_w_up[0].astype(bf16),
                 ffn1_w_down[0].astype(bf16), row(mix_norm[0]), w_in[0].astype(bf16), invf)
    x1, q, k, vt, z = _ffn1_proj(x.reshape(b * t, d), *ffn1_args,
                                 row_tile=ROW_TILE, rows_per_seq=t, pos_offset=N_META)
    meta = jnp.concatenate([meta_tokens, jnp.zeros((META_TILE - N_META, d), meta_tokens.dtype)], axis=0)
    _, _, km, vmt, zm = _ffn1_proj(meta, *ffn1_args, row_tile=META_TILE, rows_per_seq=META_TILE, pos_offset=0)

    a = _diff_attention(q.reshape(b, t, D_ATTN), k.reshape(b, t, D_ATTN), vt, km, vmt,
                        row(lambda_q1[0]), row(lambda_k1[0]), row(lambda_q2[0]), row(lambda_k2[0]),
                        row(subln_w[0]))

    zstart = jnp.concatenate([jnp.zeros((HALO - N_META, D_CONV), jnp.float32), zm[:N_META]], axis=0)
    y = _mix_ffn2(x1, a.reshape(b * t, D_ATTN), z, zstart, conv_w[0], row(conv_b[0]),
                  row(conv_ln_g[0]), row(conv_ln_b[0]), w_out[0].astype(bf16), row(ffn2_norm[0]),
                  ffn2_w_gate[0].astype(bf16), ffn2_w_up[0].astype(bf16), ffn2_w_down[0].astype(bf16),
                  row(final_norm), rows_per_seq=t)
    return y.reshape(b, t, d)
```

```python
import functools
import math

import jax
import jax.numpy as jnp
from jax import lax
from jax.experimental import pallas as pl
from jax.experimental.pallas import tpu as pltpu

D_MODEL = 1024
N_META = 16
D_ATTN = 512
D_CONV = 512
HEAD_DIM = 64
N_DIFF_HEADS = 4
CONV_WIDTH = 31
D_FF = 2816
ROPE_THETA = 10000.0
NORM_EPS = 1e-5
D_IN_PROJ = 3 * D_ATTN + 2 * D_CONV
LAMBDA_INIT = 0.8 - 0.6 * math.exp(-0.3 * 0)

LANES = 128
SUBLANES = 8
HALO = 32
ROW_TILE = 512
META_TILE = 128
Q_CHUNK = 256
FF_CHUNK = 256
CONV_ROWS = 64
TOKEN_LAG = 2
ONES_ROWS = 16
VMEM_LIMIT_BYTES = 56 * 1024 * 1024
MASK_VALUE = -1e30


def _rmsnorm(x, g):
    return x * lax.rsqrt(jnp.mean(x * x, axis=-1, keepdims=True) + NORM_EPS) * g


def _swiglu(h_bf16, wg_ref, wu_ref, wd_ref):
    g = jnp.dot(h_bf16, wg_ref[...], preferred_element_type=jnp.float32)
    u = jnp.dot(h_bf16, wu_ref[...], preferred_element_type=jnp.float32)
    a = (g * jax.nn.sigmoid(g) * u).astype(jnp.bfloat16)
    return jnp.dot(a, wd_ref[...], preferred_element_type=jnp.float32)


def _rope(x, cos, sin_lo, sin_hi):
    return x * cos + pltpu.roll(x, 96, 1) * sin_lo + pltpu.roll(x, 32, 1) * sin_hi


def _ffn1_proj_kernel(x_ref, n1_ref, wg_ref, wu_ref, wd_ref, nm_ref, win_ref, invf_ref,
                      x1_ref, q_ref, k_ref, vt_ref, z_ref, cosr_scr, sinr_scr, *, tiles_per_seq, pos_offset):
    tr = x_ref.shape[0]

    @pl.when(pl.program_id(0) == 0)
    def _():
        r = lax.broadcasted_iota(jnp.int32, (tr, LANES), 0).astype(jnp.float32)
        cosr_scr[...] = jnp.cos(r * invf_ref[...])
        sinr_scr[...] = jnp.sin(r * invf_ref[...])

    x = x_ref[...]
    h = _rmsnorm(x, n1_ref[...]).astype(jnp.bfloat16)
    x1 = x + 0.5 * _swiglu(h, wg_ref, wu_ref, wd_ref)
    x1_ref[...] = x1

    hm = _rmsnorm(x1, nm_ref[...]).astype(jnp.bfloat16)
    proj = jnp.dot(hm, win_ref[...], preferred_element_type=jnp.float32)

    t0 =((pl.program_id(0) % tiles_per_seq) * tr + pos_offset).astype(jnp.float32)
    cos0 = jnp.cos(t0 * invf_ref[...])
    sin0 = jnp.sin(t0 * invf_ref[...])
    cos = cos0 * cosr_scr[...] - sin0 * sinr_scr[...]
    sin = sin0 * cosr_scr[...] + cos0 * sinr_scr[...]
    lane = lax.broadcasted_iota(jnp.int32, (tr, LANES), 1)
    first_half = (lane % HEAD_DIM) < (HEAD_DIM // 2)
    sin_lo = jnp.where(first_half, -sin, 0.0)
    sin_hi = jnp.where(first_half, 0.0, sin)
    scale = HEAD_DIM ** -0.5 * math.log2(math.e)
    for c in range(D_ATTN // LANES):
        sl = slice(c * LANES, (c + 1) * LANES)
        qc = proj[:, c * LANES:(c + 1) * LANES]
        kc = proj[:, D_ATTN + c * LANES:D_ATTN + (c + 1) * LANES]
        q_ref[:, sl] = (_rope(qc, cos, sin_lo, sin_hi) * scale).astype(jnp.bfloat16)
        k_ref[:, sl] = _rope(kc, cos, sin_lo, sin_hi).astype(jnp.bfloat16)
    hw = 2 * HEAD_DIM
    for hd in range(N_DIFF_HEADS):
        vh = proj[:, 2 * D_ATTN + hd * hw:2 * D_ATTN + (hd + 1) * hw]
        vt_ref[0, hd, 0] = vh.T.astype(jnp.bfloat16)
    ua = proj[:, 3 * D_ATTN:3 * D_ATTN + D_CONV]
    ug = proj[:, 3 * D_ATTN + D_CONV:]
    z_ref[...] = ua * jax.nn.sigmoid(ug)


def _resident(shape):
    return pl.BlockSpec(shape, lambda i: (0,) * len(shape), pipeline_mode=pl.Buffered(1))


def _ffn1_proj(x2d, n1, wg, wu, wd, nm, win, invf, *, row_tile, rows_per_seq, pos_offset):
    rows = x2d.shape[0]
    assert rows % row_tile == 0 and rows_per_seq % row_tile == 0
    tiles_per_seq = rows_per_seq // row_tile
    hw = 2 * HEAD_DIM
    row_spec = lambda w: pl.BlockSpec((row_tile, w), lambda i: (i, 0))
    vt_spec = pl.BlockSpec((1, N_DIFF_HEADS, 1, hw, row_tile),
                           lambda i: (i // tiles_per_seq, 0, i % tiles_per_seq, 0, 0))
    kern = functools.partial(_ffn1_proj_kernel, tiles_per_seq=tiles_per_seq, pos_offset=pos_offset)
    return pl.pallas_call(
        kern,
        grid=(rows // row_tile,),
        in_specs=[row_spec(D_MODEL), _resident((1, D_MODEL)),
                  _resident((D_MODEL, D_FF)), _resident((D_MODEL, D_FF)), _resident((D_FF, D_MODEL)),
                  _resident((1, D_MODEL)), _resident((D_MODEL, D_IN_PROJ)), _resident((1, LANES))],
        out_specs=[row_spec(D_MODEL), row_spec(D_ATTN), row_spec(D_ATTN), vt_spec, row_spec(D_CONV)],
        out_shape=[jax.ShapeDtypeStruct((rows, D_MODEL), jnp.float32),
                   jax.ShapeDtypeStruct((rows, D_ATTN), jnp.bfloat16),
                   jax.ShapeDtypeStruct((rows, D_ATTN), jnp.bfloat16),
                   jax.ShapeDtypeStruct((rows // rows_per_seq, N_DIFF_HEADS, tiles_per_seq, hw, row_tile),
                                        jnp.bfloat16),
                   jax.ShapeDtypeStruct((rows, D_CONV), jnp.float32)],
        scratch_shapes=[pltpu.VMEM((row_tile, LANES), jnp.float32),
                        pltpu.VMEM((row_tile, LANES), jnp.float32)],
        compiler_params=pltpu.CompilerParams(dimension_semantics=("arbitrary",),
                                             vmem_limit_bytes=VMEM_LIMIT_BYTES),
        name="ffn1_proj",
    )(x2d, n1, wg, wu, wd, nm, win, invf)


def _attn_kernel(q_ref, qn_ref, k_ref, vt_ref, km_ref, vmt_ref, lq1_ref, lk1_ref, lq2_ref, lk2_ref, sw_ref,
                 o_ref, m_scr, acc_scr, s0_scr, s1_scr, p0_scr, p1_scr, a0_scr, a1_scr, c0_scr, c1_scr):
    tq = q_ref.shape[1]
    hw, tk = vt_ref.shape[-2:]
    qi = pl.program_id(2)
    nt = (((1,), (1,)), ((), ()))
    n_chunks = 2 * tq // Q_CHUNK

    def stack_maps(q):
        lane = lax.broadcasted_iota(jnp.int32, q.shape, 1)
        zero = jnp.zeros_like(q)
        return jnp.concatenate([jnp.where(lane < HEAD_DIM, q, zero), jnp.where(lane >= HEAD_DIM, q, zero)], axis=0)

    qs = stack_maps(q_ref[0])

    def with_ones(vt):
        return jnp.concatenate([vt, jnp.ones((ONES_ROWS, vt.shape[1]), vt.dtype)], axis=0)

    s_bufs, p_bufs, a_bufs = ((s0_scr, c0_scr), (s1_scr, c1_scr)), (p0_scr, p1_scr), (a0_scr, a1_scr)

    all_lanes = [(0, 2 * tq)]
    late_lanes = [(tk, tq), (tq + tk, 2 * tq)]

    def scores_into(bufs, j, q_stacked, lanes=all_lanes):
        s_scr, c_scr = bufs
        kb = k_ref[0, pl.ds(pl.multiple_of(j * tk, tk), tk), :]
        for lo, hi in lanes:
            s = lax.dot_general(kb, q_stacked[lo:hi], nt, preferred_element_type=jnp.float32)
            s_scr[:, lo:hi] = s
            c_scr[:, lo:hi] = jnp.max(s, axis=0, keepdims=True)

    def scores(j, par, lanes=all_lanes):
        scores_into(s_bufs[par], j, qs, lanes)

    def softmax(par, key_off=None, src=None):
        (s_scr, c_scr), p_scr, a_scr = (s_bufs[par] if src is None else src), p_bufs[par], a_bufs[par]
        for c in range(n_chunks):
            sl = slice(c * Q_CHUNK, (c + 1) * Q_CHUNK)
            rel = tk if key_off is None else (c * Q_CHUNK) % tq - key_off
            if rel <= -Q_CHUNK:
                continue
            if rel >= tk:
                nk = tk
                load = lambda: s_scr[:, sl]
                col_max = c_scr[:, sl]
            else:
                nk = min(tk, rel + Q_CHUNK)
                if nk < tk:
                    p_scr[nk:tk, sl] = jnp.zeros((tk - nk, Q_CHUNK), jnp.bfloat16)

                def load(nk=nk, rel=rel):
                    s = s_scr[0:nk, sl]
                    key = lax.broadcasted_iota(jnp.int32, s.shape, 0)
                    qry = lax.broadcasted_iota(jnp.int32, s.shape, 1) + rel
                    return jnp.where(key <= qry, s, MASK_VALUE)
                col_max = jnp.max(load(), axis=0, keepdims=True)
            m_prev = m_scr[:, sl]
            m_new = jnp.maximum(m_prev, col_max)
            a_scr[:, sl] = jnp.exp2(m_prev - m_new)
            p_scr[0:nk, sl] = jnp.exp2(load() - m_new).astype(jnp.bfloat16)
            m_scr[:, sl] = m_new

    def values(j, par, lanes=all_lanes):
        vtb = with_ones(vt_ref[0, 0, j])
        for lo, hi in lanes:
            acc_scr[:, lo:hi] = (a_bufs[par][:, lo:hi] * acc_scr[:, lo:hi]
                                 + jnp.dot(vtb, p_bufs[par][:, lo:hi], preferred_element_type=jnp.float32))

    def reset_max():
        m_scr[...] = jnp.full(m_scr.shape, MASK_VALUE, jnp.float32)

    def reset_acc():
        acc_scr[...] = jnp.zeros(acc_scr.shape, jnp.float32)

    def meta_softmax():
        s = lax.dot_general(km_ref[0:N_META, :], qs, nt, preferred_element_type=jnp.float32)
        m_prev = m_scr[...]
        m_new = jnp.maximum(m_prev, jnp.max(s, axis=0, keepdims=True))
        pad = jnp.zeros((vmt_ref.shape[-1] - N_META, 2 * tq), jnp.bfloat16)
        p_pad = jnp.concatenate([jnp.exp2(s - m_new).astype(jnp.bfloat16), pad], axis=0)
        return jnp.exp2(m_prev - m_new), p_pad

    def meta_values(alpha, p_pad):
        acc_scr[...] = alpha * acc_scr[...] + jnp.dot(with_ones(vmt_ref[0, 0, 0]), p_pad,
                                                      preferred_element_type=jnp.float32)

    @pl.when(qi == 0)
    def _():
        scores(0, 0)
        scores(1, 1, late_lanes)
        reset_max()
        reset_acc()
        softmax(0, key_off=0)

    def pair(p, carry):
        t = 2 * p + 1
        scores(t + 1, 0)
        softmax(1)
        values(t - 1, 0)
        scores(t + 2, 1)
        softmax(0)
        values(t, 1)
        return carry

    lax.fori_loop(0, jnp.maximum(qi - 1, 0), pair, 0)

    @pl.when(qi > 0)
    def _():
        scores(2 * qi, 0)
        softmax(1)
        values(2 * qi - 2, 0)
        scores(2 * qi + 1, 1, late_lanes)
        softmax(0, key_off=0)
        values(2 * qi - 1, 1)

    qs_next = stack_maps(qn_ref[0])
    scores_into(s_bufs[0], 0, qs_next)
    softmax(1, key_off=tk)
    meta_alpha, meta_p = meta_softmax()
    values(2 * qi, 0)
    values(2 * qi + 1, 1, late_lanes)
    meta_values(meta_alpha, meta_p)

    lam = (jnp.exp(jnp.sum(lq1_ref[...] * lk1_ref[...], axis=-1, keepdims=True))
           - jnp.exp(jnp.sum(lq2_ref[...] * lk2_ref[...], axis=-1, keepdims=True)) + LAMBDA_INIT)
    o = acc_scr[0:hw, :] / acc_scr[hw:hw + 1, :]
    o = o[:, :tq] - lam * o[:, tq:]
    o = o * lax.rsqrt(jnp.mean(o * o, axis=0, keepdims=True) + NORM_EPS)
    o_ref[0] = (o.T * sw_ref[...] * (1.0 - LAMBDA_INIT)).astype(o_ref.dtype)

    reset_max()
    reset_acc()
    scores_into(s_bufs[1], 1, qs_next)
    softmax(0)


def _diff_attention(q, k, vt, km, vmt, lq1, lk1, lq2, lk2, sw):
    b, t, _ = q.shape
    n_kv, hw, tk = vt.shape[2:]
    tq = 2 * tk
    assert t == n_kv * tk and t % tq == 0 and tk % Q_CHUNK == 0
    small = lambda shape: pl.BlockSpec(shape, lambda bi, hi, qi: (0,) * len(shape))
    n_q = t // tq
    return pl.pallas_call(
        _attn_kernel,
        grid=(b, N_DIFF_HEADS, n_q),
        in_specs=[pl.BlockSpec((1, tq, hw), lambda bi, hi, qi: (bi, qi, hi)),
                  pl.BlockSpec((1, tq, hw), lambda bi, hi, qi: (bi, jnp.minimum(qi + 1, n_q - 1), hi)),
                  pl.BlockSpec((1, t, hw), lambda bi, hi, qi: (bi, 0, hi)),
                  pl.BlockSpec((1, 1, n_kv, hw, tk), lambda bi, hi, qi: (bi, hi, 0, 0, 0)),
                  pl.BlockSpec((km.shape[0], hw), lambda bi, hi, qi: (0, hi)),
                  pl.BlockSpec((1, 1, 1) + vmt.shape[3:], lambda bi, hi, qi: (0, hi, 0, 0, 0)),
                  small((1, HEAD_DIM)), small((1, HEAD_DIM)), small((1, HEAD_DIM)), small((1, HEAD_DIM)),
                  small((1, hw))],
        out_specs=pl.BlockSpec((1, tq, hw), lambda bi, hi, qi: (bi, qi, hi)),
        out_shape=jax.ShapeDtypeStruct((b, t, D_ATTN), jnp.bfloat16),
        scratch_shapes=[pltpu.VMEM((1, 2 * tq), jnp.float32),
                        pltpu.VMEM((hw + ONES_ROWS, 2 * tq), jnp.float32),
                        pltpu.VMEM((tk, 2 * tq), jnp.float32),
                        pltpu.VMEM((tk, 2 * tq), jnp.float32),
                        pltpu.VMEM((tk, 2 * tq), jnp.bfloat16),
                        pltpu.VMEM((tk, 2 * tq), jnp.bfloat16),
                        pltpu.VMEM((1, 2 * tq), jnp.float32),
                        pltpu.VMEM((1, 2 * tq), jnp.float32),
                        pltpu.VMEM((1, 2 * tq), jnp.float32),
                        pltpu.VMEM((1, 2 * tq), jnp.float32)],
        compiler_params=pltpu.CompilerParams(dimension_semantics=("arbitrary", "arbitrary", "arbitrary"),
                                             vmem_limit_bytes=VMEM_LIMIT_BYTES),
        name="diff_attn",
    )(q, q, k, vt, km, vmt, lq1, lk1, lq2, lk2, sw)


def _mix_ffn2_kernel(x1_ref, a_ref, z0_ref, znext_ref, zprev_ref, zstart_ref, zero_ref, cw_ref, cb_ref, lg_ref, lb_ref,
                     wo_ref, n2_ref, wg_ref, wu_ref, wd_ref, nf_ref, y_ref, zw_scr, c_scr, *, tiles_per_seq):
    tr = x1_ref.shape[0]
    i = pl.program_id(0)

    def fill_window(z_tile_ref, halo):
        zw_scr[0:HALO, :] = halo
        zw_scr[HALO:HALO + tr, :] = z_tile_ref[...]
        zw_scr[HALO + tr:, :] = jnp.zeros((SUBLANES, D_CONV), jnp.float32)

    def conv_rows(r0):
        base = HALO - (CONV_WIDTH - 1)
        blocks = []
        for l0 in range(0, D_CONV, LANES):
            ln = slice(l0, l0 + LANES)
            conv = None
            for rho in range(SUBLANES):
                group = None
                for o in range(rho, HALO + 1, SUBLANES):
                    if o < base:
                        continue
                    lo = r0 + o - rho
                    term = cw_ref[o - base:o - base + 1, ln] * zw_scr[lo:lo + CONV_ROWS + SUBLANES, ln]
                    group = term if group is None else group + term
                shifted = group[rho:rho + CONV_ROWS]
                conv = shifted if conv is None else conv + shifted
            blocks.append(conv)
        conv = jnp.concatenate(blocks, axis=1) + cb_ref[...]
        mu = jnp.mean(conv, axis=-1, keepdims=True)
        cc = conv - mu
        var = jnp.mean(cc * cc, axis=-1, keepdims=True)
        c = cc * lax.rsqrt(var + NORM_EPS) * lg_ref[...] + lb_ref[...]
        c = c * jax.nn.sigmoid(c)
        c_scr[r0:r0 + CONV_ROWS, :] = c.astype(jnp.bfloat16)
        bits = pltpu.bitcast(c[0:SUBLANES, 0:FF_CHUNK], jnp.int32) & zero_ref[...]
        return pltpu.bitcast(bits, jnp.float32)

    conv_starts = list(range(0, tr, CONV_ROWS))

    @pl.when(i == 0)
    def _():
        fill_window(z0_ref, zstart_ref[...])
        for r0 in conv_starts:
            conv_rows(r0)

    x2 = (x1_ref[...]
          + jnp.dot(a_ref[...], wo_ref[0:D_ATTN, :], preferred_element_type=jnp.float32)
          + jnp.dot(c_scr[...], wo_ref[D_ATTN:, :], preferred_element_type=jnp.float32))
    h = _rmsnorm(x2, n2_ref[...]).astype(jnp.bfloat16)

    next_starts_seq = ((i + 1) % tiles_per_seq) == 0
    fill_window(znext_ref, jnp.where(next_starts_seq, zstart_ref[...], zprev_ref[...]))
    ffn = None
    tokens = {}
    for ci, c0 in enumerate(range(0, D_FF, FF_CHUNK)):
        cols = slice(c0, c0 + FF_CHUNK)
        g = jnp.dot(h, wg_ref[:, cols], preferred_element_type=jnp.float32)
        u = jnp.dot(h, wu_ref[:, cols], preferred_element_type=jnp.float32)
        if ci in tokens:
            g = jnp.concatenate([g[0:SUBLANES] + tokens[ci], g[SUBLANES:]], axis=0)
        act = (g * jax.nn.sigmoid(g) * u).astype(jnp.bfloat16)
        d = jnp.dot(act, wd_ref[cols, :], preferred_element_type=jnp.float32)
        ffn = d if ffn is None else ffn + d
        if ci < len(conv_starts):
            tokens[ci + TOKEN_LAG] = conv_rows(conv_starts[ci])
    x3 = x2 + 0.5 * ffn
    y_ref[...] = _rmsnorm(x3, nf_ref[...])


def _mix_ffn2(x1, a, z, zstart, cw, cb, lg, lb, wo, n2, wg, wu, wd, nf, *, rows_per_seq):
    rows = x1.shape[0]
    tr = ROW_TILE
    assert rows % tr == 0 and rows_per_seq % tr == 0 and tr % HALO == 0
    assert D_FF % FF_CHUNK == 0 and tr % CONV_ROWS == 0 and D_FF // FF_CHUNK >= tr // CONV_ROWS + TOKEN_LAG
    n_tiles = rows // tr
    row_spec = lambda w: pl.BlockSpec((tr, w), lambda i: (i, 0))
    z0_spec = pl.BlockSpec((tr, D_CONV), lambda i: (0, 0), pipeline_mode=pl.Buffered(1))
    znext_spec = pl.BlockSpec((tr, D_CONV), lambda i: (jnp.minimum(i + 1, n_tiles - 1), 0))
    halo_spec = pl.BlockSpec((HALO, D_CONV), lambda i: ((i + 1) * (tr // HALO) - 1, 0))
    kern = functools.partial(_mix_ffn2_kernel, tiles_per_seq=rows_per_seq // tr)
    return pl.pallas_call(
        kern,
        grid=(n_tiles,),
        in_specs=[row_spec(D_MODEL), row_spec(D_ATTN), z0_spec, znext_spec, halo_spec,
                  _resident((HALO, D_CONV)), _resident((SUBLANES, FF_CHUNK)),
                  _resident((CONV_WIDTH, D_CONV)), _resident((1, D_CONV)),
                  _resident((1, D_CONV)), _resident((1, D_CONV)),
                  _resident((D_ATTN + D_CONV, D_MODEL)), _resident((1, D_MODEL)),
                  _resident((D_MODEL, D_FF)), _resident((D_MODEL, D_FF)), _resident((D_FF, D_MODEL)),
                  _resident((1, D_MODEL))],
        out_specs=row_spec(D_MODEL),
        out_shape=jax.ShapeDtypeStruct((rows, D_MODEL), jnp.float32),
        scratch_shapes=[pltpu.VMEM((HALO + tr + SUBLANES, D_CONV), jnp.float32),
                        pltpu.VMEM((tr, D_CONV), jnp.bfloat16)],
        compiler_params=pltpu.CompilerParams(dimension_semantics=("arbitrary",),
                                             vmem_limit_bytes=VMEM_LIMIT_BYTES),
        name="mix_ffn2",
    )(x1, a, z, z, z, zstart, jnp.zeros((SUBLANES, FF_CHUNK), jnp.int32), cw, cb, lg, lb, wo, n2, wg, wu, wd, nf)


def kernel(x, meta_tokens, ffn1_norm, ffn1_w_gate, ffn1_w_up, ffn1_w_down, mix_norm, w_in, lambda_q1, lambda_k1, lambda_q2, lambda_k2, subln_w, conv_w, conv_b, conv_ln_g, conv_ln_b, w_out, ffn2_norm, ffn2_w_gate, ffn2_w_up, ffn2_w_down, final_norm):
    b, t, d = x.shape
    bf16 = jnp.bfloat16
    row = lambda v: v.reshape(1, -1)

    inv_freq = ROPE_THETA ** (-jnp.arange(0, HEAD_DIM, 2, dtype=jnp.float32) / HEAD_DIM)
    invf = jnp.tile(inv_freq, LANES // (HEAD_DIM // 2)).reshape(1, LANES)

    ffn1_args = (row(ffn1_norm[0]), ffn1_w_gate[0].astype(bf16), ffn1_w_up[0].astype(bf16),
                 ffn1_w_down[0].astype(bf16), row(mix_norm[0]), w_in[0].astype(bf16), invf)
    x1, q, k, vt, z = _ffn1_proj(x.reshape(b * t, d), *ffn1_args,
                                 row_tile=ROW_TILE, rows_per_seq=t, pos_offset=N_META)
    meta = jnp.concatenate([meta_tokens, jnp.zeros((META_TILE - N_META, d), meta_tokens.dtype)], axis=0)
    _, _, km, vmt, zm = _ffn1_proj(meta, *ffn1_args, row_tile=META_TILE, rows_per_seq=META_TILE, pos_offset=0)

    a = _diff_attention(q.reshape(b, t, D_ATTN), k.reshape(b, t, D_ATTN), vt, km, vmt,
                        row(lambda_q1[0]), row(lambda_k1[0]), row(lambda_q2[0]), row(lambda_k2[0]),
                        row(subln_w[0]))

    zstart = jnp.concatenate([jnp.zeros((HALO - N_META, D_CONV), jnp.float32), zm[:N_META]], axis=0)
    y = _mix_ffn2(x1, a.reshape(b * t, D_ATTN), z, zstart, conv_w[0], row(conv_b[0]),
                  row(conv_ln_g[0]), row(conv_ln_b[0]), w_out[0].astype(bf16), row(ffn2_norm[0]),
                  ffn2_w_gate[0].astype(bf16), ffn2_w_up[0].astype(bf16), ffn2_w_down[0].astype(bf16),
                  row(final_norm), rows_per_seq=t)
    return y.reshape(b, t, d)
```

```python
import functools
import math

import jax
import jax.numpy as jnp
from jax import lax
from jax.experimental import pallas as pl
from jax.experimental.pallas import tpu as pltpu

D_MODEL = 1024
N_META = 16
D_ATTN = 512
D_CONV = 512
HEAD_DIM = 64
N_DIFF_HEADS = 4
CONV_WIDTH = 31
D_FF = 2816
ROPE_THETA = 10000.0
NORM_EPS = 1e-5
D_IN_PROJ = 3 * D_ATTN + 2 * D_CONV
LAMBDA_INIT = 0.8 - 0.6 * math.exp(-0.3 * 0)

LANES = 128
SUBLANES = 8
HALO = 32
ROW_TILE = 512
META_TILE = 128
KV_PER_Q = 4
Q_CHUNK = 256
FF_CHUNK = 256
CONV_ROWS = 64
TOKEN_LAG = 2
ONES_ROWS = 16
VMEM_LIMIT_BYTES = 56 * 1024 * 1024
MASK_VALUE = -1e30


def _rmsnorm(x, g):
    return x * lax.rsqrt(jnp.mean(x * x, axis=-1, keepdims=True) + NORM_EPS) * g


def _swiglu(h_bf16, wg_ref, wu_ref, wd_ref):
    g = jnp.dot(h_bf16, wg_ref[...], preferred_element_type=jnp.float32)
    u = jnp.dot(h_bf16, wu_ref[...], preferred_element_type=jnp.float32)
    a = (g * jax.nn.sigmoid(g) * u).astype(jnp.bfloat16)
    return jnp.dot(a, wd_ref[...], preferred_element_type=jnp.float32)


def _rope(x, cos, sin_lo, sin_hi):
    return x * cos + pltpu.roll(x, 96, 1) * sin_lo + pltpu.roll(x, 32, 1) * sin_hi


def _ffn1_proj_kernel(x_ref, n1_ref, wg_ref, wu_ref, wd_ref, nm_ref, win_ref, invf_ref,
                      x1_ref, q_ref, k_ref, vt_ref, z_ref, cosr_scr, sinr_scr, *, tiles_per_seq, pos_offset):
    tr = x_ref.shape[0]

    @pl.when(pl.program_id(0) == 0)
    def _():
        r = lax.broadcasted_iota(jnp.int32, (tr, LANES), 0).astype(jnp.float32)
        cosr_scr[...] = jnp.cos(r * invf_ref[...])
        sinr_scr[...] = jnp.sin(r * invf_ref[...])

    x = x_ref[...]
    h = _rmsnorm(x, n1_ref[...]).astype(jnp.bfloat16)
    x1 = x + 0.5 * _swiglu(h, wg_ref, wu_ref, wd_ref)
    x1_ref[...] = x1

    hm = _rmsnorm(x1, nm_ref[...]).astype(jnp.bfloat16)
    proj = jnp.dot(hm, win_ref[...], preferred_element_type=jnp.float32)

    t0 =((pl.program_id(0) % tiles_per_seq) * tr + pos_offset).astype(jnp.float32)
    cos0 = jnp.cos(t0 * invf_ref[...])
    sin0 = jnp.sin(t0 * invf_ref[...])
    cos = cos0 * cosr_scr[...] - sin0 * sinr_scr[...]
    sin = sin0 * cosr_scr[...] + cos0 * sinr_scr[...]
    lane = lax.broadcasted_iota(jnp.int32, (tr, LANES), 1)
    first_half = (lane % HEAD_DIM) < (HEAD_DIM // 2)
    sin_lo = jnp.where(first_half, -sin, 0.0)
    sin_hi = jnp.where(first_half, 0.0, sin)
    scale = HEAD_DIM ** -0.5 * math.log2(math.e)
    for c in range(D_ATTN // LANES):
        sl = slice(c * LANES, (c + 1) * LANES)
        qc = proj[:, c * LANES:(c + 1) * LANES]
        kc = proj[:, D_ATTN + c * LANES:D_ATTN + (c + 1) * LANES]
        q_ref[:, sl] = (_rope(qc, cos, sin_lo, sin_hi) * scale).astype(jnp.bfloat16)
        k_ref[:, sl] = _rope(kc, cos, sin_lo, sin_hi).astype(jnp.bfloat16)
    hw = 2 * HEAD_DIM
    for hd in range(N_DIFF_HEADS):
        vh = proj[:, 2 * D_ATTN + hd * hw:2 * D_ATTN + (hd + 1) * hw]
        vt_ref[0, hd, 0] = vh.T.astype(jnp.bfloat16)
    ua = proj[:, 3 * D_ATTN:3 * D_ATTN + D_CONV]
    ug = proj[:, 3 * D_ATTN + D_CONV:]
    z_ref[...] = ua * jax.nn.sigmoid(ug)


def _resident(shape):
    return pl.BlockSpec(shape, lambda i: (0,) * len(shape), pipeline_mode=pl.Buffered(1))


def _ffn1_proj(x2d, n1, wg, wu, wd, nm, win, invf, *, row_tile, rows_per_seq, pos_offset):
    rows = x2d.shape[0]
    assert rows % row_tile == 0 and rows_per_seq % row_tile == 0
    tiles_per_seq = rows_per_seq // row_tile
    hw = 2 * HEAD_DIM
    row_spec = lambda w: pl.BlockSpec((row_tile, w), lambda i: (i, 0))
    vt_spec = pl.BlockSpec((1, N_DIFF_HEADS, 1, hw, row_tile),
                           lambda i: (i // tiles_per_seq, 0, i % tiles_per_seq, 0, 0))
    kern = functools.partial(_ffn1_proj_kernel, tiles_per_seq=tiles_per_seq, pos_offset=pos_offset)
    return pl.pallas_call(
        kern,
        grid=(rows // row_tile,),
        in_specs=[row_spec(D_MODEL), _resident((1, D_MODEL)),
                  _resident((D_MODEL, D_FF)), _resident((D_MODEL, D_FF)), _resident((D_FF, D_MODEL)),
                  _resident((1, D_MODEL)), _resident((D_MODEL, D_IN_PROJ)), _resident((1, LANES))],
        out_specs=[row_spec(D_MODEL), row_spec(D_ATTN), row_spec(D_ATTN), vt_spec, row_spec(D_CONV)],
        out_shape=[jax.ShapeDtypeStruct((rows, D_MODEL), jnp.float32),
                   jax.ShapeDtypeStruct((rows, D_ATTN), jnp.bfloat16),
                   jax.ShapeDtypeStruct((rows, D_ATTN), jnp.bfloat16),
                   jax.ShapeDtypeStruct((rows // rows_per_seq, N_DIFF_HEADS, tiles_per_seq, hw, row_tile),
                                        jnp.bfloat16),
                   jax.ShapeDtypeStruct((rows, D_CONV), jnp.float32)],
        scratch_shapes=[pltpu.VMEM((row_tile, LANES), jnp.float32),
                        pltpu.VMEM((row_tile, LANES), jnp.float32)],
        compiler_params=pltpu.CompilerParams(dimension_semantics=("arbitrary",),
                                             vmem_limit_bytes=VMEM_LIMIT_BYTES),
        name="ffn1_proj",
    )(x2d, n1, wg, wu, wd, nm, win, invf)


def _attn_kernel(q_ref, qn_ref, k_ref, vt_ref, km_ref, vmt_ref, lq1_ref, lk1_ref, lq2_ref, lk2_ref, sw_ref,
                 o_ref, m_scr, acc_scr, s0_scr, s1_scr, p0_scr, p1_scr, a0_scr, a1_scr, c0_scr, c1_scr):
    tq = q_ref.shape[1]
    hw, tk = vt_ref.shape[-2:]
    qi = pl.program_id(2)
    nt = (((1,), (1,)), ((), ()))
    n_chunks = 2 * tq // Q_CHUNK

    def stack_maps(q):
        lane = lax.broadcasted_iota(jnp.int32, q.shape, 1)
        zero = jnp.zeros_like(q)
        return jnp.concatenate([jnp.where(lane < HEAD_DIM, q, zero), jnp.where(lane >= HEAD_DIM, q, zero)], axis=0)

    qs = stack_maps(q_ref[0])

    def with_ones(vt):
        return jnp.concatenate([vt, jnp.ones((ONES_ROWS, vt.shape[1]), vt.dtype)], axis=0)

    s_bufs, p_bufs, a_bufs = ((s0_scr, c0_scr), (s1_scr, c1_scr)), (p0_scr, p1_scr), (a0_scr, a1_scr)

    all_lanes = [(0, 2 * tq)]

    def late_lanes(d):
        return all_lanes if d <= 0 else [(d * tk, tq), (tq + d * tk, 2 * tq)]

    def scores_into(bufs, j, q_stacked, lanes=all_lanes):
        s_scr, c_scr = bufs
        kb = k_ref[0, pl.ds(pl.multiple_of(j * tk, tk), tk), :]
        for lo, hi in lanes:
            s = lax.dot_general(kb, q_stacked[lo:hi], nt, preferred_element_type=jnp.float32)
            s_scr[:, lo:hi] = s
            c_scr[:, lo:hi] = jnp.max(s, axis=0, keepdims=True)

    def scores(j, par, lanes=all_lanes):
        scores_into(s_bufs[par], j, qs, lanes)

    def softmax(par, key_off=None, src=None):
        (s_scr, c_scr), p_scr, a_scr = (s_bufs[par] if src is None else src), p_bufs[par], a_bufs[par]
        for c in range(n_chunks):
            sl = slice(c * Q_CHUNK, (c + 1) * Q_CHUNK)
            rel = tk if key_off is None else (c * Q_CHUNK) % tq - key_off
            if rel <= -Q_CHUNK:
                continue
            if rel >= tk:
                nk = tk
                load = lambda: s_scr[:, sl]
                col_max = c_scr[:, sl]
            else:
                nk = min(tk, rel + Q_CHUNK)
                if nk < tk:
                    p_scr[nk:tk, sl] = jnp.zeros((tk - nk, Q_CHUNK), jnp.bfloat16)

                def load(nk=nk, rel=rel):
                    s = s_scr[0:nk, sl]
                    key = lax.broadcasted_iota(jnp.int32, s.shape, 0)
                    qry = lax.broadcasted_iota(jnp.int32, s.shape, 1) + rel
                    return jnp.where(key <= qry, s, MASK_VALUE)
                col_max = jnp.max(load(), axis=0, keepdims=True)
            m_prev = m_scr[:, sl]
            m_new = jnp.maximum(m_prev, col_max)
            a_scr[:, sl] = jnp.exp2(m_prev - m_new)
            p_scr[0:nk, sl] = jnp.exp2(load() - m_new).astype(jnp.bfloat16)
            m_scr[:, sl] = m_new

    def values(j, par, lanes=all_lanes):
        vtb = with_ones(vt_ref[0, 0, j])
        for lo, hi in lanes:
            acc_scr[:, lo:hi] = (a_bufs[par][:, lo:hi] * acc_scr[:, lo:hi]
                                 + jnp.dot(vtb, p_bufs[par][:, lo:hi], preferred_element_type=jnp.float32))

    def reset_max():
        m_scr[...] = jnp.full(m_scr.shape, MASK_VALUE, jnp.float32)

    def reset_acc():
        acc_scr[...] = jnp.zeros(acc_scr.shape, jnp.float32)

    def meta_softmax():
        s = lax.dot_general(km_ref[0:N_META, :], qs, nt, preferred_element_type=jnp.float32)
        m_prev = m_scr[...]
        m_new = jnp.maximum(m_prev, jnp.max(s, axis=0, keepdims=True))
        pad = jnp.zeros((vmt_ref.shape[-1] - N_META, 2 * tq), jnp.bfloat16)
        p_pad = jnp.concatenate([jnp.exp2(s - m_new).astype(jnp.bfloat16), pad], axis=0)
        return jnp.exp2(m_prev - m_new), p_pad

    def meta_values(alpha, p_pad):
        acc_scr[...] = alpha * acc_scr[...] + jnp.dot(with_ones(vmt_ref[0, 0, 0]), p_pad,
                                                      preferred_element_type=jnp.float32)

    n_diag = tq // tk
    first_diag = n_diag * qi

    def diag_step(d):
        par = d % 2
        if d + 1 < n_diag:
            scores(first_diag + d + 1, 1 - par, late_lanes(d + 1))
        softmax(par, key_off=d * tk)
        values(first_diag + d - 1, 1 - par, late_lanes(d - 1))

    @pl.when(qi == 0)
    def _():
        scores(0, 0)
        scores(1, 1, late_lanes(1))
        reset_max()
        reset_acc()
        softmax(0, key_off=0)

    def pair(p, carry):
        t = 2 * p + 1
        scores(t + 1, 0)
        softmax(1)
        values(t - 1, 0)
        scores(t + 2, 1)
        softmax(0)
        values(t, 1)
        return carry

    lax.fori_loop(0, jnp.maximum(first_diag // 2 - 1, 0), pair, 0)

    @pl.when(qi > 0)
    def _():
        scores(first_diag, 0)
        softmax(1)
        values(first_diag - 2, 0)
        scores(first_diag + 1, 1, late_lanes(1))
        softmax(0, key_off=0)
        values(first_diag - 1, 1)

    for d in range(1, n_diag - 1):
        diag_step(d)

    qs_next = stack_maps(qn_ref[0])
    scores_into(s_bufs[0], 0, qs_next)
    diag_step(n_diag - 1)
    meta_alpha, meta_p = meta_softmax()
    values(first_diag + n_diag - 1, 1, late_lanes(n_diag - 1))
    meta_values(meta_alpha, meta_p)

    lam = (jnp.exp(jnp.sum(lq1_ref[...] * lk1_ref[...], axis=-1, keepdims=True))
           - jnp.exp(jnp.sum(lq2_ref[...] * lk2_ref[...], axis=-1, keepdims=True)) + LAMBDA_INIT)
    o = acc_scr[0:hw, :] / acc_scr[hw:hw + 1, :]
    o = o[:, :tq] - lam * o[:, tq:]
    o = o * lax.rsqrt(jnp.mean(o * o, axis=0, keepdims=True) + NORM_EPS)
    o_ref[0] = (o.T * sw_ref[...] * (1.0 - LAMBDA_INIT)).astype(o_ref.dtype)

    reset_max()
    reset_acc()
    scores_into(s_bufs[1], 1, qs_next)
    softmax(0)


def _diff_attention(q, k, vt, km, vmt, lq1, lk1, lq2, lk2, sw):
    b, t, _ = q.shape
    n_kv, hw, tk = vt.shape[2:]
    tq = KV_PER_Q * tk
    assert KV_PER_Q % 2 == 0 and t == n_kv * tk and t % tq == 0 and tk % Q_CHUNK == 0
    small = lambda shape: pl.BlockSpec(shape, lambda bi, hi, qi: (0,) * len(shape))
    n_q = t // tq
    return pl.pallas_call(
        _attn_kernel,
        grid=(b, N_DIFF_HEADS, n_q),
        in_specs=[pl.BlockSpec((1, tq, hw), lambda bi, hi, qi: (bi, qi, hi)),
                  pl.BlockSpec((1, tq, hw), lambda bi, hi, qi: (bi, jnp.minimum(qi + 1, n_q - 1), hi)),
                  pl.BlockSpec((1, t, hw), lambda bi, hi, qi: (bi, 0, hi)),
                  pl.BlockSpec((1, 1, n_kv, hw, tk), lambda bi, hi, qi: (bi, hi, 0, 0, 0)),
                  pl.BlockSpec((km.shape[0], hw), lambda bi, hi, qi: (0, hi)),
                  pl.BlockSpec((1, 1, 1) + vmt.shape[3:], lambda bi, hi, qi: (0, hi, 0, 0, 0)),
                  small((1, HEAD_DIM)), small((1, HEAD_DIM)), small((1, HEAD_DIM)), small((1, HEAD_DIM)),
                  small((1, hw))],
        out_specs=pl.BlockSpec((1, tq, hw), lambda bi, hi, qi: (bi, qi, hi)),
        out_shape=jax.ShapeDtypeStruct((b, t, D_ATTN), jnp.bfloat16),
        scratch_shapes=[pltpu.VMEM((1, 2 * tq), jnp.float32),
                        pltpu.VMEM((hw + ONES_ROWS, 2 * tq), jnp.float32),
                        pltpu.VMEM((tk, 2 * tq), jnp.float32),
                        pltpu.VMEM((tk, 2 * tq), jnp.float32),
                        pltpu.VMEM((tk, 2 * tq), jnp.bfloat16),
                        pltpu.VMEM((tk, 2 * tq), jnp.bfloat16),
                        pltpu.VMEM((1, 2 * tq), jnp.float32),
                        pltpu.VMEM((1, 2 * tq), jnp.float32),
                        pltpu.VMEM((1, 2 * tq), jnp.float32),
                        pltpu.VMEM((1, 2 * tq), jnp.float32)],
        compiler_params=pltpu.CompilerParams(dimension_semantics=("arbitrary", "arbitrary", "arbitrary"),
                                             vmem_limit_bytes=VMEM_LIMIT_BYTES),
        name="diff_attn",
    )(q, q, k, vt, km, vmt, lq1, lk1, lq2, lk2, sw)


def _mix_ffn2_kernel(x1_ref, a_ref, z0_ref, znext_ref, zprev_ref, zstart_ref, zero_ref, cw_ref, cb_ref, lg_ref, lb_ref,
                     wo_ref, n2_ref, wg_ref, wu_ref, wd_ref, nf_ref, y_ref, zw_scr, c_scr, *, tiles_per_seq):
    tr = x1_ref.shape[0]
    i = pl.program_id(0)

    def fill_window(z_tile_ref, halo):
        zw_scr[0:HALO, :] = halo
        zw_scr[HALO:HALO + tr, :] = z_tile_ref[...]
        zw_scr[HALO + tr:, :] = jnp.zeros((SUBLANES, D_CONV), jnp.float32)

    def conv_rows(r0):
        base = HALO - (CONV_WIDTH - 1)
        blocks = []
        for l0 in range(0, D_CONV, LANES):
            ln = slice(l0, l0 + LANES)
            conv = None
            for rho in range(SUBLANES):
                group = None
                for o in range(rho, HALO + 1, SUBLANES):
                    if o < base:
                        continue
                    lo = r0 + o - rho
                    term = cw_ref[o - base:o - base + 1, ln] * zw_scr[lo:lo + CONV_ROWS + SUBLANES, ln]
                    group = term if group is None else group + term
                shifted = group[rho:rho + CONV_ROWS]
                conv = shifted if conv is None else conv + shifted
            blocks.append(conv)
        conv = jnp.concatenate(blocks, axis=1) + cb_ref[...]
        mu = jnp.mean(conv, axis=-1, keepdims=True)
        cc = conv - mu
        var = jnp.mean(cc * cc, axis=-1, keepdims=True)
        c = cc * lax.rsqrt(var + NORM_EPS) * lg_ref[...] + lb_ref[...]
        c = c * jax.nn.sigmoid(c)
        c_scr[r0:r0 + CONV_ROWS, :] = c.astype(jnp.bfloat16)
        bits = pltpu.bitcast(c[0:SUBLANES, 0:FF_CHUNK], jnp.int32) & zero_ref[...]
        return pltpu.bitcast(bits, jnp.float32)

    conv_starts = list(range(0, tr, CONV_ROWS))

    @pl.when(i == 0)
    def _():
        fill_window(z0_ref, zstart_ref[...])
        for r0 in conv_starts:
            conv_rows(r0)

    x2 = (x1_ref[...]
          + jnp.dot(a_ref[...], wo_ref[0:D_ATTN, :], preferred_element_type=jnp.float32)
          + jnp.dot(c_scr[...], wo_ref[D_ATTN:, :], preferred_element_type=jnp.float32))
    h = _rmsnorm(x2, n2_ref[...]).astype(jnp.bfloat16)

    next_starts_seq = ((i + 1) % tiles_per_seq) == 0
    fill_window(znext_ref, jnp.where(next_starts_seq, zstart_ref[...], zprev_ref[...]))
    ffn = None
    tokens = {}
    for ci, c0 in enumerate(range(0, D_FF, FF_CHUNK)):
        cols = slice(c0, c0 + FF_CHUNK)
        g = jnp.dot(h, wg_ref[:, cols], preferred_element_type=jnp.float32)
        u = jnp.dot(h, wu_ref[:, cols], preferred_element_type=jnp.float32)
        if ci in tokens:
            g = jnp.concatenate([g[0:SUBLANES] + tokens[ci], g[SUBLANES:]], axis=0)
        act = (g * jax.nn.sigmoid(g) * u).astype(jnp.bfloat16)
        d = jnp.dot(act, wd_ref[cols, :], preferred_element_type=jnp.float32)
        ffn = d if ffn is None else ffn + d
        if ci < len(conv_starts):
            tokens[ci + TOKEN_LAG] = conv_rows(conv_starts[ci])
    x3 = x2 + 0.5 * ffn
    y_ref[...] = _rmsnorm(x3, nf_ref[...])


def _mix_ffn2(x1, a, z, zstart, cw, cb, lg, lb, wo, n2, wg, wu, wd, nf, *, rows_per_seq):
    rows = x1.shape[0]
    tr = ROW_TILE
    assert rows % tr == 0 and rows_per_seq % tr == 0 and tr % HALO == 0
    assert D_FF % FF_CHUNK == 0 and tr % CONV_ROWS == 0 and D_FF // FF_CHUNK >= tr // CONV_ROWS + TOKEN_LAG
    n_tiles = rows // tr
    row_spec = lambda w: pl.BlockSpec((tr, w), lambda i: (i, 0))
    z0_spec = pl.BlockSpec((tr, D_CONV), lambda i: (0, 0), pipeline_mode=pl.Buffered(1))
    znext_spec = pl.BlockSpec((tr, D_CONV), lambda i: (jnp.minimum(i + 1, n_tiles - 1), 0))
    halo_spec = pl.BlockSpec((HALO, D_CONV), lambda i: ((i + 1) * (tr // HALO) - 1, 0))
    kern = functools.partial(_mix_ffn2_kernel, tiles_per_seq=rows_per_seq // tr)
    return pl.pallas_call(
        kern,
        grid=(n_tiles,),
        in_specs=[row_spec(D_MODEL), row_spec(D_ATTN), z0_spec, znext_spec, halo_spec,
                  _resident((HALO, D_CONV)), _resident((SUBLANES, FF_CHUNK)),
                  _resident((CONV_WIDTH, D_CONV)), _resident((1, D_CONV)),
                  _resident((1, D_CONV)), _resident((1, D_CONV)),
                  _resident((D_ATTN + D_CONV, D_MODEL)), _resident((1, D_MODEL)),
                  _resident((D_MODEL, D_FF)), _resident((D_MODEL, D_FF)), _resident((D_FF, D_MODEL)),
                  _resident((1, D_MODEL))],
        out_specs=row_spec(D_MODEL),
        out_shape=jax.ShapeDtypeStruct((rows, D_MODEL), jnp.float32),
        scratch_shapes=[pltpu.VMEM((HALO + tr + SUBLANES, D_CONV), jnp.float32),
                        pltpu.VMEM((tr, D_CONV), jnp.bfloat16)],
        compiler_params=pltpu.CompilerParams(dimension_semantics=("arbitrary",),
                                             vmem_limit_bytes=VMEM_LIMIT_BYTES),
        name="mix_ffn2",
    )(x1, a, z, z, z, zstart, jnp.zeros((SUBLANES, FF_CHUNK), jnp.int32), cw, cb, lg, lb, wo, n2, wg, wu, wd, nf)


def kernel(x, meta_tokens, ffn1_norm, ffn1_w_gate, ffn1_w_up, ffn1_w_down, mix_norm, w_in, lambda_q1, lambda_k1, lambda_q2, lambda_k2, subln_w, conv_w, conv_b, conv_ln_g, conv_ln_b, w_out, ffn2_norm, ffn2_w_gate, ffn2_w_up, ffn2_w_down, final_norm):
    b, t, d = x.shape
    bf16 = jnp.bfloat16
    row = lambda v: v.reshape(1, -1)

    inv_freq = ROPE_THETA ** (-jnp.arange(0, HEAD_DIM, 2, dtype=jnp.float32) / HEAD_DIM)
    invf = jnp.tile(inv_freq, LANES // (HEAD_DIM // 2)).reshape(1, LANES)

    ffn1_args = (row(ffn1_norm[0]), ffn1_w_gate[0].astype(bf16), ffn1_w_up[0].astype(bf16),
                 ffn1_w_down[0].astype(bf16), row(mix_norm[0]), w_in[0].astype(bf16), invf)
    x1, q, k, vt, z = _ffn1_proj(x.reshape(b * t, d), *ffn1_args,
                                 row_tile=ROW_TILE, rows_per_seq=t, pos_offset=N_META)
    meta = jnp.concatenate([meta_tokens, jnp.zeros((META_TILE - N_META, d), meta_tokens.dtype)], axis=0)
    _, _, km, vmt, zm = _ffn1_proj(meta, *ffn1_args, row_tile=META_TILE, rows_per_seq=META_TILE, pos_offset=0)

    a = _diff_attention(q.reshape(b, t, D_ATTN), k.reshape(b, t, D_ATTN), vt, km, vmt,
                        row(lambda_q1[0]), row(lambda_k1[0]), row(lambda_q2[0]), row(lambda_k2[0]),
                        row(subln_w[0]))

    zstart = jnp.concatenate([jnp.zeros((HALO - N_META, D_CONV), jnp.float32), zm[:N_META]], axis=0)
    y = _mix_ffn2(x1, a.reshape(b * t, D_ATTN), z, zstart, conv_w[0], row(conv_b[0]),
                  row(conv_ln_g[0]), row(conv_ln_b[0]), w_out[0].astype(bf16), row(ffn2_norm[0]),
                  ffn2_w_gate[0].astype(bf16), ffn2_w_up[0].astype(bf16), ffn2_w_down[0].astype(bf16),
                  row(final_norm), rows_per_seq=t)
    return y.reshape(b, t, d)
```

```python
import functools
import math

import jax
import jax.numpy as jnp
from jax import lax
from jax.experimental import pallas as pl
from jax.experimental.pallas import tpu as pltpu

D_MODEL = 1024
N_META = 16
D_ATTN = 512
D_CONV = 512
HEAD_DIM = 64
N_DIFF_HEADS = 4
CONV_WIDTH = 31
D_FF = 2816
ROPE_THETA = 10000.0
NORM_EPS = 1e-5
D_IN_PROJ = 3 * D_ATTN + 2 * D_CONV
LAMBDA_INIT = 0.8 - 0.6 * math.exp(-0.3 * 0)

LANES = 128
SUBLANES = 8
HALO = 32
ROW_TILE = 512
META_TILE = 128
KV_PER_Q = 4
Q_CHUNK = 256
FF_CHUNK = 256
CONV_ROWS = 64
TOKEN_LAG = 2
ONES_ROWS = 16
VMEM_LIMIT_BYTES = 56 * 1024 * 1024
MASK_VALUE = -1e30


def _rmsnorm(x, g):
    return x * lax.rsqrt(jnp.mean(x * x, axis=-1, keepdims=True) + NORM_EPS) * g


def _swiglu(h_bf16, wg_ref, wu_ref, wd_ref):
    g = jnp.dot(h_bf16, wg_ref[...], preferred_element_type=jnp.float32)
    u = jnp.dot(h_bf16, wu_ref[...], preferred_element_type=jnp.float32)
    a = (g * jax.nn.sigmoid(g) * u).astype(jnp.bfloat16)
    return jnp.dot(a, wd_ref[...], preferred_element_type=jnp.float32)


def _rope(x, cos, sin_lo, sin_hi):
    return x * cos + pltpu.roll(x, 96, 1) * sin_lo + pltpu.roll(x, 32, 1) * sin_hi


def _ffn1_proj_kernel(x_ref, n1_ref, wg_ref, wu_ref, wd_ref, nm_ref, win_ref, invf_ref,
                      x1_ref, q_ref, k_ref, vt_ref, z_ref, cosr_scr, sinr_scr, *, tiles_per_seq, pos_offset):
    tr = x_ref.shape[0]

    @pl.when(pl.program_id(0) == 0)
    def _():
        r = lax.broadcasted_iota(jnp.int32, (tr, LANES), 0).astype(jnp.float32)
        cosr_scr[...] = jnp.cos(r * invf_ref[...])
        sinr_scr[...] = jnp.sin(r * invf_ref[...])

    x = x_ref[...]
    h = _rmsnorm(x, n1_ref[...]).astype(jnp.bfloat16)
    x1 = x + 0.5 * _swiglu(h, wg_ref, wu_ref, wd_ref)
    x1_ref[...] = x1

    hm = _rmsnorm(x1, nm_ref[...]).astype(jnp.bfloat16)
    proj = jnp.dot(hm, win_ref[...], preferred_element_type=jnp.float32)

    t0 =((pl.program_id(0) % tiles_per_seq) * tr + pos_offset).astype(jnp.float32)
    cos0 = jnp.cos(t0 * invf_ref[...])
    sin0 = jnp.sin(t0 * invf_ref[...])
    cos = cos0 * cosr_scr[...] - sin0 * sinr_scr[...]
    sin = sin0 * cosr_scr[...] + cos0 * sinr_scr[...]
    lane = lax.broadcasted_iota(jnp.int32, (tr, LANES), 1)
    first_half = (lane % HEAD_DIM) < (HEAD_DIM // 2)
    sin_lo = jnp.where(first_half, -sin, 0.0)
    sin_hi = jnp.where(first_half, 0.0, sin)
    scale = HEAD_DIM ** -0.5 * math.log2(math.e)
    for c in range(D_ATTN // LANES):
        sl = slice(c * LANES, (c + 1) * LANES)
        qc = proj[:, c * LANES:(c + 1) * LANES]
        kc = proj[:, D_ATTN + c * LANES:D_ATTN + (c + 1) * LANES]
        q_ref[:, sl] = (_rope(qc, cos, sin_lo, sin_hi) * scale).astype(jnp.bfloat16)
        k_ref[:, sl] = _rope(kc, cos, sin_lo, sin_hi).astype(jnp.bfloat16)
    hw = 2 * HEAD_DIM
    for hd in range(N_DIFF_HEADS):
        vh = proj[:, 2 * D_ATTN + hd * hw:2 * D_ATTN + (hd + 1) * hw]
        vt_ref[0, hd, 0] = vh.T.astype(jnp.bfloat16)
    ua = proj[:, 3 * D_ATTN:3 * D_ATTN + D_CONV]
    ug = proj[:, 3 * D_ATTN + D_CONV:]
    z_ref[...] = ua * jax.nn.sigmoid(ug)


def _resident(shape):
    return pl.BlockSpec(shape, lambda i: (0,) * len(shape), pipeline_mode=pl.Buffered(1))


def _ffn1_proj(x2d, n1, wg, wu, wd, nm, win, invf, *, row_tile, rows_per_seq, pos_offset):
    rows = x2d.shape[0]
    assert rows % row_tile == 0 and rows_per_seq % row_tile == 0
    tiles_per_seq = rows_per_seq // row_tile
    hw = 2 * HEAD_DIM
    row_spec = lambda w: pl.BlockSpec((row_tile, w), lambda i: (i, 0))
    vt_spec = pl.BlockSpec((1, N_DIFF_HEADS, 1, hw, row_tile),
                           lambda i: (i // tiles_per_seq, 0, i % tiles_per_seq, 0, 0))
    kern = functools.partial(_ffn1_proj_kernel, tiles_per_seq=tiles_per_seq, pos_offset=pos_offset)
    return pl.pallas_call(
        kern,
        grid=(rows // row_tile,),
        in_specs=[row_spec(D_MODEL), _resident((1, D_MODEL)),
                  _resident((D_MODEL, D_FF)), _resident((D_MODEL, D_FF)), _resident((D_FF, D_MODEL)),
                  _resident((1, D_MODEL)), _resident((D_MODEL, D_IN_PROJ)), _resident((1, LANES))],
        out_specs=[row_spec(D_MODEL), row_spec(D_ATTN), row_spec(D_ATTN), vt_spec, row_spec(D_CONV)],
        out_shape=[jax.ShapeDtypeStruct((rows, D_MODEL), jnp.float32),
                   jax.ShapeDtypeStruct((rows, D_ATTN), jnp.bfloat16),
                   jax.ShapeDtypeStruct((rows, D_ATTN), jnp.bfloat16),
                   jax.ShapeDtypeStruct((rows // rows_per_seq, N_DIFF_HEADS, tiles_per_seq, hw, row_tile),
                                        jnp.bfloat16),
                   jax.ShapeDtypeStruct((rows, D_CONV), jnp.float32)],
        scratch_shapes=[pltpu.VMEM((row_tile, LANES), jnp.float32),
                        pltpu.VMEM((row_tile, LANES), jnp.float32)],
        compiler_params=pltpu.CompilerParams(dimension_semantics=("arbitrary",),
                                             vmem_limit_bytes=VMEM_LIMIT_BYTES),
        name="ffn1_proj",
    )(x2d, n1, wg, wu, wd, nm, win, invf)


def _attn_kernel(q_ref, qn_ref, k_ref, vt_ref, km_ref, vmt_ref, lq1_ref, lk1_ref, lq2_ref, lk2_ref, sw_ref,
                 o_ref, m_scr, acc_scr, s0_scr, s1_scr, p0_scr, p1_scr, a0_scr, a1_scr, c0_scr, c1_scr):
    tq = q_ref.shape[1]
    hw, tk = vt_ref.shape[-2:]
    qi = pl.program_id(2)
    nt = (((1,), (1,)), ((), ()))
    n_chunks = 2 * tq // Q_CHUNK

    def stack_maps(q):
        lane = lax.broadcasted_iota(jnp.int32, q.shape, 1)
        zero = jnp.zeros_like(q)
        return jnp.concatenate([jnp.where(lane < HEAD_DIM, q, zero), jnp.where(lane >= HEAD_DIM, q, zero)], axis=0)

    qs = stack_maps(q_ref[0])

    def with_ones(vt):
        return jnp.concatenate([vt, jnp.ones((ONES_ROWS, vt.shape[1]), vt.dtype)], axis=0)

    s_bufs, p_bufs, a_bufs = ((s0_scr, c0_scr), (s1_scr, c1_scr)), (p0_scr, p1_scr), (a0_scr, a1_scr)

    all_lanes = [(0, 2 * tq)]

    def late_lanes(d):
        return all_lanes if d <= 0 else [(d * tk, tq), (tq + d * tk, 2 * tq)]

    def scores_into(bufs, j, q_stacked, lanes=all_lanes):
        s_scr, c_scr = bufs
        kb = k_ref[0, pl.ds(pl.multiple_of(j * tk, tk), tk), :]
        for lo, hi in lanes:
            s = lax.dot_general(kb, q_stacked[lo:hi], nt, preferred_element_type=jnp.float32)
            s_scr[:, lo:hi] = s
            c_scr[:, lo:hi] = jnp.max(s, axis=0, keepdims=True)

    def scores(j, par, lanes=all_lanes):
        scores_into(s_bufs[par], j, qs, lanes)

    def softmax(par, key_off=None, src=None):
        (s_scr, c_scr), p_scr, a_scr = (s_bufs[par] if src is None else src), p_bufs[par], a_bufs[par]
        for c in range(n_chunks):
            sl = slice(c * Q_CHUNK, (c + 1) * Q_CHUNK)
            rel = tk if key_off is None else (c * Q_CHUNK) % tq - key_off
            if rel <= -Q_CHUNK:
                continue
            if rel >= tk:
                nk = tk
                load = lambda: s_scr[:, sl]
                col_max = c_scr[:, sl]
            else:
                nk = min(tk, rel + Q_CHUNK)
                if nk < tk:
                    p_scr[nk:tk, sl] = jnp.zeros((tk - nk, Q_CHUNK), jnp.bfloat16)

                def load(nk=nk, rel=rel):
                    s = s_scr[0:nk, sl]
                    key = lax.broadcasted_iota(jnp.int32, s.shape, 0)
                    qry = lax.broadcasted_iota(jnp.int32, s.shape, 1) + rel
                    return jnp.where(key <= qry, s, MASK_VALUE)
                col_max = jnp.max(load(), axis=0, keepdims=True)
            m_prev = m_scr[:, sl]
            m_new = jnp.maximum(m_prev, col_max)
            a_scr[:, sl] = jnp.exp2(m_prev - m_new)
            p_scr[0:nk, sl] = jnp.exp2(load() - m_new).astype(jnp.bfloat16)
            m_scr[:, sl] = m_new

    def values(j, par, lanes=all_lanes):
        vtb = with_ones(vt_ref[0, 0, j])
        for lo, hi in lanes:
            acc_scr[:, lo:hi] = (a_bufs[par][:, lo:hi] * acc_scr[:, lo:hi]
                                 + jnp.dot(vtb, p_bufs[par][:, lo:hi], preferred_element_type=jnp.float32))

    def reset_max():
        m_scr[...] = jnp.full(m_scr.shape, MASK_VALUE, jnp.float32)

    def reset_acc():
        acc_scr[...] = jnp.zeros(acc_scr.shape, jnp.float32)

    def meta_softmax():
        s = lax.dot_general(km_ref[0:N_META, :], qs, nt, preferred_element_type=jnp.float32)
        m_prev = m_scr[...]
        m_new = jnp.maximum(m_prev, jnp.max(s, axis=0, keepdims=True))
        pad = jnp.zeros((vmt_ref.shape[-1] - N_META, 2 * tq), jnp.bfloat16)
        p_pad = jnp.concatenate([jnp.exp2(s - m_new).astype(jnp.bfloat16), pad], axis=0)
        return jnp.exp2(m_prev - m_new), p_pad

    def meta_values(alpha, p_pad):
        acc_scr[...] = alpha * acc_scr[...] + jnp.dot(with_ones(vmt_ref[0, 0, 0]), p_pad,
                                                      preferred_element_type=jnp.float32)

    n_diag = tq // tk
    first_diag = n_diag * qi

    def diag_step(d):
        par = d % 2
        if d + 1 < n_diag:
            scores(first_diag + d + 1, 1 - par, late_lanes(d + 1))
        softmax(par, key_off=d * tk)
        values(first_diag + d - 1, 1 - par, late_lanes(d - 1))

    @pl.when(qi == 0)
    def _():
        scores(0, 0)
        scores(1, 1, late_lanes(1))
        reset_max()
        reset_acc()
        softmax(0, key_off=0)

    def pair(p, carry):
        t = 2 * p + 1
        scores(t + 1, 0)
        softmax(1)
        values(t - 1, 0)
        scores(t + 2, 1)
        softmax(0)
        values(t, 1)
        return carry

    lax.fori_loop(0, jnp.maximum(first_diag // 2 - 1, 0), pair, 0)

    @pl.when(qi > 0)
    def _():
        scores(first_diag, 0)
        softmax(1)
        values(first_diag - 2, 0)
        scores(first_diag + 1, 1, late_lanes(1))
        softmax(0, key_off=0)
        values(first_diag - 1, 1)

    def last_stages(with_next):
        for d in range(1, n_diag - 1):
            diag_step(d)
        if with_next:
            qs_next = stack_maps(qn_ref[0])
            scores_into(s_bufs[0], 0, qs_next)
        diag_step(n_diag - 1)
        meta_alpha, meta_p = meta_softmax()
        values(first_diag + n_diag - 1, 1, late_lanes(n_diag - 1))
        meta_values(meta_alpha, meta_p)

        lam = (jnp.exp(jnp.sum(lq1_ref[...] * lk1_ref[...], axis=-1, keepdims=True))
               - jnp.exp(jnp.sum(lq2_ref[...] * lk2_ref[...], axis=-1, keepdims=True)) + LAMBDA_INIT)
        o = acc_scr[0:hw, :] / acc_scr[hw:hw + 1, :]
        o = o[:, :tq] - lam * o[:, tq:]
        o = o * lax.rsqrt(jnp.mean(o * o, axis=0, keepdims=True) + NORM_EPS)
        o_ref[0] = (o.T * sw_ref[...] * (1.0 - LAMBDA_INIT)).astype(o_ref.dtype)

        if with_next:
            reset_max()
            reset_acc()
            scores_into(s_bufs[1], 1, qs_next)
            softmax(0)

    is_last = qi == pl.num_programs(2) - 1

    @pl.when(jnp.logical_not(is_last))
    def _():
        last_stages(with_next=True)

    @pl.when(is_last)
    def _():
        last_stages(with_next=False)


def _diff_attention(q, k, vt, km, vmt, lq1, lk1, lq2, lk2, sw):
    b, t, _ = q.shape
    n_kv, hw, tk = vt.shape[2:]
    tq = KV_PER_Q * tk
    assert KV_PER_Q % 2 == 0 and t == n_kv * tk and t % tq == 0 and tk % Q_CHUNK == 0
    small = lambda shape: pl.BlockSpec(shape, lambda bi, hi, qi: (0,) * len(shape))
    n_q = t // tq
    return pl.pallas_call(
        _attn_kernel,
        grid=(b, N_DIFF_HEADS, n_q),
        in_specs=[pl.BlockSpec((1, tq, hw), lambda bi, hi, qi: (bi, qi, hi)),
                  pl.BlockSpec((1, tq, hw), lambda bi, hi, qi: (bi, jnp.minimum(qi + 1, n_q - 1), hi)),
                  pl.BlockSpec((1, t, hw), lambda bi, hi, qi: (bi, 0, hi)),
                  pl.BlockSpec((1, 1, n_kv, hw, tk), lambda bi, hi, qi: (bi, hi, 0, 0, 0)),
                  pl.BlockSpec((km.shape[0], hw), lambda bi, hi, qi: (0, hi)),
                  pl.BlockSpec((1, 1, 1) + vmt.shape[3:], lambda bi, hi, qi: (0, hi, 0, 0, 0)),
                  small((1, HEAD_DIM)), small((1, HEAD_DIM)), small((1, HEAD_DIM)), small((1, HEAD_DIM)),
                  small((1, hw))],
        out_specs=pl.BlockSpec((1, tq, hw), lambda bi, hi, qi: (bi, qi, hi)),
        out_shape=jax.ShapeDtypeStruct((b, t, D_ATTN), jnp.bfloat16),
        scratch_shapes=[pltpu.VMEM((1, 2 * tq), jnp.float32),
                        pltpu.VMEM((hw + ONES_ROWS, 2 * tq), jnp.float32),
                        pltpu.VMEM((tk, 2 * tq), jnp.float32),
                        pltpu.VMEM((tk, 2 * tq), jnp.float32),
                        pltpu.VMEM((tk, 2 * tq), jnp.bfloat16),
                        pltpu.VMEM((tk, 2 * tq), jnp.bfloat16),
                        pltpu.VMEM((1, 2 * tq), jnp.float32),
                        pltpu.VMEM((1, 2 * tq), jnp.float32),
                        pltpu.VMEM((1, 2 * tq), jnp.float32),
                        pltpu.VMEM((1, 2 * tq), jnp.float32)],
        compiler_params=pltpu.CompilerParams(dimension_semantics=("arbitrary", "arbitrary", "arbitrary"),
                                             vmem_limit_bytes=VMEM_LIMIT_BYTES),
        name="diff_attn",
    )(q, q, k, vt, km, vmt, lq1, lk1, lq2, lk2, sw)


def _mix_ffn2_kernel(x1_ref, a_ref, z0_ref, znext_ref, zprev_ref, zstart_ref, zero_ref, cw_ref, cb_ref, lg_ref, lb_ref,
                     wo_ref, n2_ref, wg_ref, wu_ref, wd_ref, nf_ref, y_ref, zw_scr, c_scr, *, tiles_per_seq):
    tr = x1_ref.shape[0]
    i = pl.program_id(0)

    def fill_window(z_tile_ref, halo):
        zw_scr[0:HALO, :] = halo
        zw_scr[HALO:HALO + tr, :] = z_tile_ref[...]
        zw_scr[HALO + tr:, :] = jnp.zeros((SUBLANES, D_CONV), jnp.float32)

    def conv_rows(r0):
        base = HALO - (CONV_WIDTH - 1)
        blocks = []
        for l0 in range(0, D_CONV, LANES):
            ln = slice(l0, l0 + LANES)
            conv = None
            for rho in range(SUBLANES):
                group = None
                for o in range(rho, HALO + 1, SUBLANES):
                    if o < base:
                        continue
                    lo = r0 + o - rho
                    term = cw_ref[o - base:o - base + 1, ln] * zw_scr[lo:lo + CONV_ROWS + SUBLANES, ln]
                    group = term if group is None else group + term
                shifted = group[rho:rho + CONV_ROWS]
                conv = shifted if conv is None else conv + shifted
            blocks.append(conv)
        conv = jnp.concatenate(blocks, axis=1) + cb_ref[...]
        mu = jnp.mean(conv, axis=-1, keepdims=True)
        cc = conv - mu
        var = jnp.mean(cc * cc, axis=-1, keepdims=True)
        c = cc * lax.rsqrt(var + NORM_EPS) * lg_ref[...] + lb_ref[...]
        c = c * jax.nn.sigmoid(c)
        c_scr[r0:r0 + CONV_ROWS, :] = c.astype(jnp.bfloat16)
        bits = pltpu.bitcast(c[0:SUBLANES, 0:FF_CHUNK], jnp.int32) & zero_ref[...]
        return pltpu.bitcast(bits, jnp.float32)

    conv_starts = list(range(0, tr, CONV_ROWS))

    @pl.when(i == 0)
    def _():
        fill_window(z0_ref, zstart_ref[...])
        for r0 in conv_starts:
            conv_rows(r0)

    x2 = (x1_ref[...]
          + jnp.dot(a_ref[...], wo_ref[0:D_ATTN, :], preferred_element_type=jnp.float32)
          + jnp.dot(c_scr[...], wo_ref[D_ATTN:, :], preferred_element_type=jnp.float32))
    h = _rmsnorm(x2, n2_ref[...]).astype(jnp.bfloat16)

    next_starts_seq = ((i + 1) % tiles_per_seq) == 0
    fill_window(znext_ref, jnp.where(next_starts_seq, zstart_ref[...], zprev_ref[...]))
    ffn = None
    tokens = {}
    for ci, c0 in enumerate(range(0, D_FF, FF_CHUNK)):
        cols = slice(c0, c0 + FF_CHUNK)
        g = jnp.dot(h, wg_ref[:, cols], preferred_element_type=jnp.float32)
        u = jnp.dot(h, wu_ref[:, cols], preferred_element_type=jnp.float32)
        if ci in tokens:
            g = jnp.concatenate([g[0:SUBLANES] + tokens[ci], g[SUBLANES:]], axis=0)
        act = (g * jax.nn.sigmoid(g) * u).astype(jnp.bfloat16)
        d = jnp.dot(act, wd_ref[cols, :], preferred_element_type=jnp.float32)
        ffn = d if ffn is None else ffn + d
        if ci < len(conv_starts):
            tokens[ci + TOKEN_LAG] = conv_rows(conv_starts[ci])
    x3 = x2 + 0.5 * ffn
    y_ref[...] = _rmsnorm(x3, nf_ref[...])


def _mix_ffn2(x1, a, z, zstart, cw, cb, lg, lb, wo, n2, wg, wu, wd, nf, *, rows_per_seq):
    rows = x1.shape[0]
    tr = ROW_TILE
    assert rows % tr == 0 and rows_per_seq % tr == 0 and tr % HALO == 0
    assert D_FF % FF_CHUNK == 0 and tr % CONV_ROWS == 0 and D_FF // FF_CHUNK >= tr // CONV_ROWS + TOKEN_LAG
    n_tiles = rows // tr
    row_spec = lambda w: pl.BlockSpec((tr, w), lambda i: (i, 0))
    z0_spec = pl.BlockSpec((tr, D_CONV), lambda i: (0, 0), pipeline_mode=pl.Buffered(1))
    znext_spec = pl.BlockSpec((tr, D_CONV), lambda i: (jnp.minimum(i + 1, n_tiles - 1), 0))
    halo_spec = pl.BlockSpec((HALO, D_CONV), lambda i: ((i + 1) * (tr // HALO) - 1, 0))
    kern = functools.partial(_mix_ffn2_kernel, tiles_per_seq=rows_per_seq // tr)
    return pl.pallas_call(
        kern,
        grid=(n_tiles,),
        in_specs=[row_spec(D_MODEL), row_spec(D_ATTN), z0_spec, znext_spec, halo_spec,
                  _resident((HALO, D_CONV)), _resident((SUBLANES, FF_CHUNK)),
                  _resident((CONV_WIDTH, D_CONV)), _resident((1, D_CONV)),
                  _resident((1, D_CONV)), _resident((1, D_CONV)),
                  _resident((D_ATTN + D_CONV, D_MODEL)), _resident((1, D_MODEL)),
                  _resident((D_MODEL, D_FF)), _resident((D_MODEL, D_FF)), _resident((D_FF, D_MODEL)),
                  _resident((1, D_MODEL))],
        out_specs=row_spec(D_MODEL),
        out_shape=jax.ShapeDtypeStruct((rows, D_MODEL), jnp.float32),
        scratch_shapes=[pltpu.VMEM((HALO + tr + SUBLANES, D_CONV), jnp.float32),
                        pltpu.VMEM((tr, D_CONV), jnp.bfloat16)],
        compiler_params=pltpu.CompilerParams(dimension_semantics=("arbitrary",),
                                             vmem_limit_bytes=VMEM_LIMIT_BYTES),
        name="mix_ffn2",
    )(x1, a, z, z, z, zstart, jnp.zeros((SUBLANES, FF_CHUNK), jnp.int32), cw, cb, lg, lb, wo, n2, wg, wu, wd, nf)


def kernel(x, meta_tokens, ffn1_norm, ffn1_w_gate, ffn1_w_up, ffn1_w_down, mix_norm, w_in, lambda_q1, lambda_k1, lambda_q2, lambda_k2, subln_w, conv_w, conv_b, conv_ln_g, conv_ln_b, w_out, ffn2_norm, ffn2_w_gate, ffn2_w_up, ffn2_w_down, final_norm):
    b, t, d = x.shape
    bf16 = jnp.bfloat16
    row = lambda v: v.reshape(1, -1)

    inv_freq = ROPE_THETA ** (-jnp.arange(0, HEAD_DIM, 2, dtype=jnp.float32) / HEAD_DIM)
    invf = jnp.tile(inv_freq, LANES // (HEAD_DIM // 2)).reshape(1, LANES)

    ffn1_args = (row(ffn1_norm[0]), ffn1_w_gate[0].astype(bf16), ffn1_w_up[0].astype(bf16),
                 ffn1_w_down[0].astype(bf16), row(mix_norm[0]), w_in[0].astype(bf16), invf)
    x1, q, k, vt, z = _ffn1_proj(x.reshape(b * t, d), *ffn1_args,
                                 row_tile=ROW_TILE, rows_per_seq=t, pos_offset=N_META)
    meta = jnp.concatenate([meta_tokens, jnp.zeros((META_TILE - N_META, d), meta_tokens.dtype)], axis=0)
    _, _, km, vmt, zm = _ffn1_proj(meta, *ffn1_args, row_tile=META_TILE, rows_per_seq=META_TILE, pos_offset=0)

    a = _diff_attention(q.reshape(b, t, D_ATTN), k.reshape(b, t, D_ATTN), vt, km, vmt,
                        row(lambda_q1[0]), row(lambda_k1[0]), row(lambda_q2[0]), row(lambda_k2[0]),
                        row(subln_w[0]))

    zstart = jnp.concatenate([jnp.zeros((HALO - N_META, D_CONV), jnp.float32), zm[:N_META]], axis=0)
    y = _mix_ffn2(x1, a.reshape(b * t, D_ATTN), z, zstart, conv_w[0], row(conv_b[0]),
                  row(conv_ln_g[0]), row(conv_ln_b[0]), w_out[0].astype(bf16), row(ffn2_norm[0]),
                  ffn2_w_gate[0].astype(bf16), ffn2_w_up[0].astype(bf16), ffn2_w_down[0].astype(bf16),
                  row(final_norm), rows_per_seq=t)
    return y.reshape(b, t, d)
```

```python
import functools
import math

import jax
import jax.numpy as jnp
from jax import lax
from jax.experimental import pallas as pl
from jax.experimental.pallas import tpu as pltpu

D_MODEL = 1024
N_META = 16
D_ATTN = 512
D_CONV = 512
HEAD_DIM = 64
N_DIFF_HEADS = 4
CONV_WIDTH = 31
D_FF = 2816
ROPE_THETA = 10000.0
NORM_EPS = 1e-5
D_IN_PROJ = 3 * D_ATTN + 2 * D_CONV
LAMBDA_INIT = 0.8 - 0.6 * math.exp(-0.3 * 0)

LANES = 128
SUBLANES = 8
HALO = 32
ROW_TILE = 512
META_TILE = 128
KV_PER_Q = 4
Q_CHUNK = 256
FF_CHUNK = 256
CONV_ROWS = 64
TOKEN_LAG = 2
ONES_ROWS = 16
VMEM_LIMIT_BYTES = 56 * 1024 * 1024
MASK_VALUE = -1e30


def _rmsnorm(x, g):
    return x * lax.rsqrt(jnp.mean(x * x, axis=-1, keepdims=True) + NORM_EPS) * g


def _swiglu(h_bf16, wg_ref, wu_ref, wd_ref):
    g = jnp.dot(h_bf16, wg_ref[...], preferred_element_type=jnp.float32)
    u = jnp.dot(h_bf16, wu_ref[...], preferred_element_type=jnp.float32)
    a = (g * jax.nn.sigmoid(g) * u).astype(jnp.bfloat16)
    return jnp.dot(a, wd_ref[...], preferred_element_type=jnp.float32)


def _rope(x, cos, sin_lo, sin_hi):
    return x * cos + pltpu.roll(x, 96, 1) * sin_lo + pltpu.roll(x, 32, 1) * sin_hi


def _ffn1_proj_kernel(x_ref, n1_ref, wg_ref, wu_ref, wd_ref, nm_ref, win_ref, invf_ref,
                      x1_ref, q_ref, k_ref, vt_ref, z_ref, cosr_scr, sinr_scr, *, tiles_per_seq, pos_offset):
    tr = x_ref.shape[0]

    @pl.when(pl.program_id(0) == 0)
    def _():
        r = lax.broadcasted_iota(jnp.int32, (tr, LANES), 0).astype(jnp.float32)
        cosr_scr[...] = jnp.cos(r * invf_ref[...])
        sinr_scr[...] = jnp.sin(r * invf_ref[...])

    x = x_ref[...]
    h = _rmsnorm(x, n1_ref[...]).astype(jnp.bfloat16)
    x1 = x + 0.5 * _swiglu(h, wg_ref, wu_ref, wd_ref)
    x1_ref[...] = x1

    hm = _rmsnorm(x1, nm_ref[...]).astype(jnp.bfloat16)
    proj = jnp.dot(hm, win_ref[...], preferred_element_type=jnp.float32)

    t0 =((pl.program_id(0) % tiles_per_seq) * tr + pos_offset).astype(jnp.float32)
    cos0 = jnp.cos(t0 * invf_ref[...])
    sin0 = jnp.sin(t0 * invf_ref[...])
    cos = cos0 * cosr_scr[...] - sin0 * sinr_scr[...]
    sin = sin0 * cosr_scr[...] + cos0 * sinr_scr[...]
    lane = lax.broadcasted_iota(jnp.int32, (tr, LANES), 1)
    first_half = (lane % HEAD_DIM) < (HEAD_DIM // 2)
    sin_lo = jnp.where(first_half, -sin, 0.0)
    sin_hi = jnp.where(first_half, 0.0, sin)
    scale = HEAD_DIM ** -0.5 * math.log2(math.e)
    for c in range(D_ATTN // LANES):
        sl = slice(c * LANES, (c + 1) * LANES)
        qc = proj[:, c * LANES:(c + 1) * LANES]
        kc = proj[:, D_ATTN + c * LANES:D_ATTN + (c + 1) * LANES]
        q_ref[:, sl] = (_rope(qc, cos, sin_lo, sin_hi) * scale).astype(jnp.bfloat16)
        k_ref[:, sl] = _rope(kc, cos, sin_lo, sin_hi).astype(jnp.bfloat16)
    hw = 2 * HEAD_DIM
    for hd in range(N_DIFF_HEADS):
        vh = proj[:, 2 * D_ATTN + hd * hw:2 * D_ATTN + (hd + 1) * hw]
        vt_ref[0, hd, 0] = vh.T.astype(jnp.bfloat16)
    ua = proj[:, 3 * D_ATTN:3 * D_ATTN + D_CONV]
    ug = proj[:, 3 * D_ATTN + D_CONV:]
    z_ref[...] = ua * jax.nn.sigmoid(ug)


def _resident(shape):
    return pl.BlockSpec(shape, lambda i: (0,) * len(shape), pipeline_mode=pl.Buffered(1))


def _ffn1_proj(x2d, n1, wg, wu, wd, nm, win, invf, *, row_tile, rows_per_seq, pos_offset):
    rows = x2d.shape[0]
    assert rows % row_tile == 0 and rows_per_seq % row_tile == 0
    tiles_per_seq = rows_per_seq // row_tile
    hw = 2 * HEAD_DIM
    row_spec = lambda w: pl.BlockSpec((row_tile, w), lambda i: (i, 0))
    vt_spec = pl.BlockSpec((1, N_DIFF_HEADS, 1, hw, row_tile),
                           lambda i: (i // tiles_per_seq, 0, i % tiles_per_seq, 0, 0))
    kern = functools.partial(_ffn1_proj_kernel, tiles_per_seq=tiles_per_seq, pos_offset=pos_offset)
    return pl.pallas_call(
        kern,
        grid=(rows // row_tile,),
        in_specs=[row_spec(D_MODEL), _resident((1, D_MODEL)),
                  _resident((D_MODEL, D_FF)), _resident((D_MODEL, D_FF)), _resident((D_FF, D_MODEL)),
                  _resident((1, D_MODEL)), _resident((D_MODEL, D_IN_PROJ)), _resident((1, LANES))],
        out_specs=[row_spec(D_MODEL), row_spec(D_ATTN), row_spec(D_ATTN), vt_spec, row_spec(D_CONV)],
        out_shape=[jax.ShapeDtypeStruct((rows, D_MODEL), jnp.float32),
                   jax.ShapeDtypeStruct((rows, D_ATTN), jnp.bfloat16),
                   jax.ShapeDtypeStruct((rows, D_ATTN), jnp.bfloat16),
                   jax.ShapeDtypeStruct((rows // rows_per_seq, N_DIFF_HEADS, tiles_per_seq, hw, row_tile),
                                        jnp.bfloat16),
                   jax.ShapeDtypeStruct((rows, D_CONV), jnp.float32)],
        scratch_shapes=[pltpu.VMEM((row_tile, LANES), jnp.float32),
                        pltpu.VMEM((row_tile, LANES), jnp.float32)],
        compiler_params=pltpu.CompilerParams(dimension_semantics=("arbitrary",),
                                             vmem_limit_bytes=VMEM_LIMIT_BYTES),
        name="ffn1_proj",
    )(x2d, n1, wg, wu, wd, nm, win, invf)


def _attn_kernel(q_ref, qn_ref, k_ref, vt_ref, km_ref, vmt_ref, lq1_ref, lk1_ref, lq2_ref, lk2_ref, sw_ref,
                 o_ref, m_scr, acc_scr, s0_scr, s1_scr, p0_scr, p1_scr, a0_scr, a1_scr, c0_scr, c1_scr):
    tq = q_ref.shape[1]
    hw, tk = vt_ref.shape[-2:]
    qi = pl.program_id(2)
    nt = (((1,), (1,)), ((), ()))
    n_chunks = 2 * tq // Q_CHUNK

    def stack_maps(q):
        lane = lax.broadcasted_iota(jnp.int32, q.shape, 1)
        zero = jnp.zeros_like(q)
        return jnp.concatenate([jnp.where(lane < HEAD_DIM, q, zero), jnp.where(lane >= HEAD_DIM, q, zero)], axis=0)

    qs = stack_maps(q_ref[0])

    def with_ones(vt):
        return jnp.concatenate([vt, jnp.ones((ONES_ROWS, vt.shape[1]), vt.dtype)], axis=0)

    s_bufs, p_bufs, a_bufs = ((s0_scr, c0_scr), (s1_scr, c1_scr)), (p0_scr, p1_scr), (a0_scr, a1_scr)

    all_lanes = [(0, 2 * tq)]

    def late_lanes(d):
        return all_lanes if d <= 0 else [(d * tk, tq), (tq + d * tk, 2 * tq)]

    def scores_into(bufs, j, q_stacked, lanes=all_lanes):
        s_scr, c_scr = bufs
        kb = k_ref[0, pl.ds(pl.multiple_of(j * tk, tk), tk), :]
        for lo, hi in lanes:
            s = lax.dot_general(kb, q_stacked[lo:hi], nt, preferred_element_type=jnp.float32)
            s_scr[:, lo:hi] = s
            c_scr[:, lo:hi] = jnp.max(s, axis=0, keepdims=True)

    def scores(j, par, lanes=all_lanes):
        scores_into(s_bufs[par], j, qs, lanes)

    def softmax(par, key_off=None, src=None):
        (s_scr, c_scr), p_scr, a_scr = (s_bufs[par] if src is None else src), p_bufs[par], a_bufs[par]
        for c in range(n_chunks):
            sl = slice(c * Q_CHUNK, (c + 1) * Q_CHUNK)
            rel = tk if key_off is None else (c * Q_CHUNK) % tq - key_off
            if rel <= -Q_CHUNK:
                continue
            if rel >= tk:
                nk = tk
                load = lambda: s_scr[:, sl]
                col_max = c_scr[:, sl]
            else:
                nk = min(tk, rel + Q_CHUNK)
                if nk < tk:
                    p_scr[nk:tk, sl] = jnp.zeros((tk - nk, Q_CHUNK), jnp.bfloat16)

                def load(nk=nk, rel=rel):
                    s = s_scr[0:nk, sl]
                    key = lax.broadcasted_iota(jnp.int32, s.shape, 0)
                    qry = lax.broadcasted_iota(jnp.int32, s.shape, 1) + rel
                    return jnp.where(key <= qry, s, MASK_VALUE)
                col_max = jnp.max(load(), axis=0, keepdims=True)
            m_prev = m_scr[:, sl]
            m_new = jnp.maximum(m_prev, col_max)
            a_scr[:, sl] = jnp.exp2(m_prev - m_new)
            p_scr[0:nk, sl] = jnp.exp2(load() - m_new).astype(jnp.bfloat16)
            m_scr[:, sl] = m_new

    def values(j, par, lanes=all_lanes):
        vtb = with_ones(vt_ref[0, 0, j])
        for lo, hi in lanes:
            acc_scr[:, lo:hi] = (a_bufs[par][:, lo:hi] * acc_scr[:, lo:hi]
                                 + jnp.dot(vtb, p_bufs[par][:, lo:hi], preferred_element_type=jnp.float32))

    def reset_max():
        m_scr[...] = jnp.full(m_scr.shape, MASK_VALUE, jnp.float32)

    def reset_acc():
        acc_scr[...] = jnp.zeros(acc_scr.shape, jnp.float32)

    def meta_softmax():
        s = lax.dot_general(km_ref[0:N_META, :], qs, nt, preferred_element_type=jnp.float32)
        m_prev = m_scr[...]
        m_new = jnp.maximum(m_prev, jnp.max(s, axis=0, keepdims=True))
        pad = jnp.zeros((vmt_ref.shape[-1] - N_META, 2 * tq), jnp.bfloat16)
        p_pad = jnp.concatenate([jnp.exp2(s - m_new).astype(jnp.bfloat16), pad], axis=0)
        return jnp.exp2(m_prev - m_new), p_pad

    def meta_values(alpha, p_pad):
        acc_scr[...] = alpha * acc_scr[...] + jnp.dot(with_ones(vmt_ref[0, 0, 0]), p_pad,
                                                      preferred_element_type=jnp.float32)

    n_diag = tq // tk
    first_diag = n_diag * qi

    def diag_step(d):
        par = d % 2
        if d + 1 < n_diag:
            scores(first_diag + d + 1, 1 - par, late_lanes(d + 1))
        softmax(par, key_off=d * tk)
        values(first_diag + d - 1, 1 - par, late_lanes(d - 1))

    @pl.when(qi == 0)
    def _():
        scores(0, 0)
        scores(1, 1, late_lanes(1))
        reset_max()
        reset_acc()
        softmax(0, key_off=0)

    def pair(p, carry):
        t = 2 * p + 1
        scores(t + 1, 0)
        softmax(1)
        values(t - 1, 0)
        scores(t + 2, 1)
        softmax(0)
        values(t, 1)
        return carry

    lax.fori_loop(0, jnp.maximum(first_diag // 2 - 1, 0), pair, 0)

    @pl.when(qi > 0)
    def _():
        scores(first_diag, 0)
        softmax(1)
        values(first_diag - 2, 0)
        scores(first_diag + 1, 1, late_lanes(1))
        softmax(0, key_off=0)
        values(first_diag - 1, 1)

    def last_stages(with_next):
        for d in range(1, n_diag - 1):
            diag_step(d)
        if with_next:
            qs_next = stack_maps(qn_ref[0])
            scores_into(s_bufs[0], 0, qs_next)
        diag_step(n_diag - 1)
        meta_alpha, meta_p = meta_softmax()
        values(first_diag + n_diag - 1, 1, late_lanes(n_diag - 1))
        meta_values(meta_alpha, meta_p)

        lam = (jnp.exp(jnp.sum(lq1_ref[...] * lk1_ref[...], axis=-1, keepdims=True))
               - jnp.exp(jnp.sum(lq2_ref[...] * lk2_ref[...], axis=-1, keepdims=True)) + LAMBDA_INIT)
        o = acc_scr[0:hw, :] / acc_scr[hw:hw + 1, :]
        o = o[:, :tq] - lam * o[:, tq:]
        o = o * lax.rsqrt(jnp.mean(o * o, axis=0, keepdims=True) + NORM_EPS)
        o_ref[0] = (o.T * sw_ref[...] * (1.0 - LAMBDA_INIT)).astype(o_ref.dtype)

        if with_next:
            reset_max()
            reset_acc()
            scores_into(s_bufs[1], 1, qs_next)
            softmax(0)

    is_last = qi == pl.num_programs(2) - 1

    @pl.when(jnp.logical_not(is_last))
    def _():
        last_stages(with_next=True)

    @pl.when(is_last)
    def _():
        last_stages(with_next=False)


def _diff_attention(q, k, vt, km, vmt, lq1, lk1, lq2, lk2, sw):
    b, t, _ = q.shape
    n_kv, hw, tk = vt.shape[2:]
    tq = KV_PER_Q * tk
    assert KV_PER_Q % 2 == 0 and t == n_kv * tk and t % tq == 0 and tk % Q_CHUNK == 0
    small = lambda shape: pl.BlockSpec(shape, lambda bi, hi, qi: (0,) * len(shape))
    n_q = t // tq
    return pl.pallas_call(
        _attn_kernel,
        grid=(b, N_DIFF_HEADS, n_q),
        in_specs=[pl.BlockSpec((1, tq, hw), lambda bi, hi, qi: (bi, qi, hi)),
                  pl.BlockSpec((1, tq, hw), lambda bi, hi, qi: (bi, jnp.minimum(qi + 1, n_q - 1), hi)),
                  pl.BlockSpec((1, t, hw), lambda bi, hi, qi: (bi, 0, hi)),
                  pl.BlockSpec((1, 1, n_kv, hw, tk), lambda bi, hi, qi: (bi, hi, 0, 0, 0)),
                  pl.BlockSpec((km.shape[0], hw), lambda bi, hi, qi: (0, hi)),
                  pl.BlockSpec((1, 1, 1) + vmt.shape[3:], lambda bi, hi, qi: (0, hi, 0, 0, 0)),
                  small((1, HEAD_DIM)), small((1, HEAD_DIM)), small((1, HEAD_DIM)), small((1, HEAD_DIM)),
                  small((1, hw))],
        out_specs=pl.BlockSpec((1, tq, hw), lambda bi, hi, qi: (bi, qi, hi)),
        out_shape=jax.ShapeDtypeStruct((b, t, D_ATTN), jnp.bfloat16),
        scratch_shapes=[pltpu.VMEM((1, 2 * tq), jnp.float32),
                        pltpu.VMEM((hw + ONES_ROWS, 2 * tq), jnp.float32),
                        pltpu.VMEM((tk, 2 * tq), jnp.float32),
                        pltpu.VMEM((tk, 2 * tq), jnp.float32),
                        pltpu.VMEM((tk, 2 * tq), jnp.bfloat16),
                        pltpu.VMEM((tk, 2 * tq), jnp.bfloat16),
                        pltpu.VMEM((1, 2 * tq), jnp.float32),
                        pltpu.VMEM((1, 2 * tq), jnp.float32),
                        pltpu.VMEM((1, 2 * tq), jnp.float32),
                        pltpu.VMEM((1, 2 * tq), jnp.float32)],
        compiler_params=pltpu.CompilerParams(dimension_semantics=("arbitrary", "arbitrary", "arbitrary"),
                                             vmem_limit_bytes=VMEM_LIMIT_BYTES),
        name="diff_attn",
    )(q, q, k, vt, km, vmt, lq1, lk1, lq2, lk2, sw)


def _mix_ffn2_kernel(x10_ref, a0_ref, x1n_ref, an_ref, z0_ref, znext_ref, zprev_ref, zstart_ref, zero_ref,
                     cw_ref, cb_ref, lg_ref, lb_ref, wo_ref, n2_ref, wg_ref, wu_ref, wd_ref, nf_ref,
                     y_ref, zw_scr, c_scr, x2_scr, h_scr, *, tiles_per_seq):
    tr = y_ref.shape[0]
    i = pl.program_id(0)

    def fill_window(z_tile_ref, halo):
        zw_scr[0:HALO, :] = halo
        zw_scr[HALO:HALO + tr, :] = z_tile_ref[...]
        zw_scr[HALO + tr:, :] = jnp.zeros((SUBLANES, D_CONV), jnp.float32)

    def conv_rows(r0):
        base = HALO - (CONV_WIDTH - 1)
        blocks = []
        for l0 in range(0, D_CONV, LANES):
            ln = slice(l0, l0 + LANES)
            conv = None
            for rho in range(SUBLANES):
                group = None
                for o in range(rho, HALO + 1, SUBLANES):
                    if o < base:
                        continue
                    lo = r0 + o - rho
                    term = cw_ref[o - base:o - base + 1, ln] * zw_scr[lo:lo + CONV_ROWS + SUBLANES, ln]
                    group = term if group is None else group + term
                shifted = group[rho:rho + CONV_ROWS]
                conv = shifted if conv is None else conv + shifted
            blocks.append(conv)
        conv = jnp.concatenate(blocks, axis=1) + cb_ref[...]
        mu = jnp.mean(conv, axis=-1, keepdims=True)
        cc = conv - mu
        var = jnp.mean(cc * cc, axis=-1, keepdims=True)
        c = cc * lax.rsqrt(var + NORM_EPS) * lg_ref[...] + lb_ref[...]
        c = c * jax.nn.sigmoid(c)
        c_scr[r0:r0 + CONV_ROWS, :] = c.astype(jnp.bfloat16)
        bits = pltpu.bitcast(c[0:SUBLANES, 0:FF_CHUNK], jnp.int32) & zero_ref[...]
        return pltpu.bitcast(bits, jnp.float32)

    conv_starts = list(range(0, tr, CONV_ROWS))

    def mix_residual(x1_tile_ref, a_tile_ref):
        x2 = (x1_tile_ref[...]
              + jnp.dot(a_tile_ref[...], wo_ref[0:D_ATTN, :], preferred_element_type=jnp.float32)
              + jnp.dot(c_scr[...], wo_ref[D_ATTN:, :], preferred_element_type=jnp.float32))
        x2_scr[...] = x2
        h_scr[...] = _rmsnorm(x2, n2_ref[...]).astype(jnp.bfloat16)

    @pl.when(i == 0)
    def _():
        fill_window(z0_ref, zstart_ref[...])
        for r0 in conv_starts:
            conv_rows(r0)
        mix_residual(x10_ref, a0_ref)

    h = h_scr[...]

    next_starts_seq = ((i + 1) % tiles_per_seq) == 0
    fill_window(znext_ref, jnp.where(next_starts_seq, zstart_ref[...], zprev_ref[...]))
    ffn = None
    tokens = {}
    for ci, c0 in enumerate(range(0, D_FF, FF_CHUNK)):
        cols = slice(c0, c0 + FF_CHUNK)
        g = jnp.dot(h, wg_ref[:, cols], preferred_element_type=jnp.float32)
        u = jnp.dot(h, wu_ref[:, cols], preferred_element_type=jnp.float32)
        if ci in tokens:
            g = jnp.concatenate([g[0:SUBLANES] + tokens[ci], g[SUBLANES:]], axis=0)
        act = (g * jax.nn.sigmoid(g) * u).astype(jnp.bfloat16)
        d = jnp.dot(act, wd_ref[cols, :], preferred_element_type=jnp.float32)
        ffn = d if ffn is None else ffn + d
        if ci < len(conv_starts):
            tokens[ci + TOKEN_LAG] = conv_rows(conv_starts[ci])
    x3 = x2_scr[...] + 0.5 * ffn
    y_ref[...] = _rmsnorm(x3, nf_ref[...])
    mix_residual(x1n_ref, an_ref)


def _mix_ffn2(x1, a, z, zstart, cw, cb, lg, lb, wo, n2, wg, wu, wd, nf, *, rows_per_seq):
    rows = x1.shape[0]
    tr = ROW_TILE
    assert rows % tr == 0 and rows_per_seq % tr == 0 and tr % HALO == 0
    assert D_FF % FF_CHUNK == 0 and tr % CONV_ROWS == 0 and D_FF // FF_CHUNK >= tr // CONV_ROWS + TOKEN_LAG
    n_tiles = rows // tr
    row_spec = lambda w: pl.BlockSpec((tr, w), lambda i: (i, 0))
    first_spec = lambda w: pl.BlockSpec((tr, w), lambda i: (0, 0), pipeline_mode=pl.Buffered(1))
    next_spec = lambda w: pl.BlockSpec((tr, w), lambda i: (jnp.minimum(i + 1, n_tiles - 1), 0))
    z0_spec, znext_spec = first_spec(D_CONV), next_spec(D_CONV)
    halo_spec = pl.BlockSpec((HALO, D_CONV), lambda i: ((i + 1) * (tr // HALO) - 1, 0))
    kern = functools.partial(_mix_ffn2_kernel, tiles_per_seq=rows_per_seq // tr)
    return pl.pallas_call(
        kern,
        grid=(n_tiles,),
        in_specs=[first_spec(D_MODEL), first_spec(D_ATTN), next_spec(D_MODEL), next_spec(D_ATTN),
                  z0_spec, znext_spec, halo_spec,
                  _resident((HALO, D_CONV)), _resident((SUBLANES, FF_CHUNK)),
                  _resident((CONV_WIDTH, D_CONV)), _resident((1, D_CONV)),
                  _resident((1, D_CONV)), _resident((1, D_CONV)),
                  _resident((D_ATTN + D_CONV, D_MODEL)), _resident((1, D_MODEL)),
                  _resident((D_MODEL, D_FF)), _resident((D_MODEL, D_FF)), _resident((D_FF, D_MODEL)),
                  _resident((1, D_MODEL))],
        out_specs=row_spec(D_MODEL),
        out_shape=jax.ShapeDtypeStruct((rows, D_MODEL), jnp.float32),
        scratch_shapes=[pltpu.VMEM((HALO + tr + SUBLANES, D_CONV), jnp.float32),
                        pltpu.VMEM((tr, D_CONV), jnp.bfloat16),
                        pltpu.VMEM((tr, D_MODEL), jnp.float32),
                        pltpu.VMEM((tr, D_MODEL), jnp.bfloat16)],
        compiler_params=pltpu.CompilerParams(dimension_semantics=("arbitrary",),
                                             vmem_limit_bytes=VMEM_LIMIT_BYTES),
        name="mix_ffn2",
    )(x1, a, x1, a, z, z, z, zstart, jnp.zeros((SUBLANES, FF_CHUNK), jnp.int32), cw, cb, lg, lb, wo, n2, wg, wu, wd, nf)


def kernel(x, meta_tokens, ffn1_norm, ffn1_w_gate, ffn1_w_up, ffn1_w_down, mix_norm, w_in, lambda_q1, lambda_k1, lambda_q2, lambda_k2, subln_w, conv_w, conv_b, conv_ln_g, conv_ln_b, w_out, ffn2_norm, ffn2_w_gate, ffn2_w_up, ffn2_w_down, final_norm):
    b, t, d = x.shape
    bf16 = jnp.bfloat16
    row = lambda v: v.reshape(1, -1)

    inv_freq = ROPE_THETA ** (-jnp.arange(0, HEAD_DIM, 2, dtype=jnp.float32) / HEAD_DIM)
    invf = jnp.tile(inv_freq, LANES // (HEAD_DIM // 2)).reshape(1, LANES)

    ffn1_args = (row(ffn1_norm[0]), ffn1_w_gate[0].astype(bf16), ffn1_w_up[0].astype(bf16),
                 ffn1_w_down[0].astype(bf16), row(mix_norm[0]), w_in[0].astype(bf16), invf)
    x1, q, k, vt, z = _ffn1_proj(x.reshape(b * t, d), *ffn1_args,
                                 row_tile=ROW_TILE, rows_per_seq=t, pos_offset=N_META)
    meta = jnp.concatenate([meta_tokens, jnp.zeros((META_TILE - N_META, d), meta_tokens.dtype)], axis=0)
    _, _, km, vmt, zm = _ffn1_proj(meta, *ffn1_args, row_tile=META_TILE, rows_per_seq=META_TILE, pos_offset=0)

    a = _diff_attention(q.reshape(b, t, D_ATTN), k.reshape(b, t, D_ATTN), vt, km, vmt,
                        row(lambda_q1[0]), row(lambda_k1[0]), row(lambda_q2[0]), row(lambda_k2[0]),
                        row(subln_w[0]))

    zstart = jnp.concatenate([jnp.zeros((HALO - N_META, D_CONV), jnp.float32), zm[:N_META]], axis=0)
    y = _mix_ffn2(x1, a.reshape(b * t, D_ATTN), z, zstart, conv_w[0], row(conv_b[0]),
                  row(conv_ln_g[0]), row(conv_ln_b[0]), w_out[0].astype(bf16), row(ffn2_norm[0]),
                  ffn2_w_gate[0].astype(bf16), ffn2_w_up[0].astype(bf16), ffn2_w_down[0].astype(bf16),
                  row(final_norm), rows_per_seq=t)
    return y.reshape(b, t, d)
```

```python
import functools
import math

import jax
import jax.numpy as jnp
from jax import lax
from jax.experimental import pallas as pl
from jax.experimental.pallas import tpu as pltpu

D_MODEL = 1024
N_META = 16
D_ATTN = 512
D_CONV = 512
HEAD_DIM = 64
N_DIFF_HEADS = 4
CONV_WIDTH = 31
D_FF = 2816
ROPE_THETA = 10000.0
NORM_EPS = 1e-5
D_IN_PROJ = 3 * D_ATTN + 2 * D_CONV
LAMBDA_INIT = 0.8 - 0.6 * math.exp(-0.3 * 0)

LANES = 128
SUBLANES = 8
HALO = 32
ROW_TILE = 512
META_TILE = 128
KV_PER_Q = 4
Q_CHUNK = 256
FF_CHUNK = 256
CONV_ROWS = 64
TOKEN_LAG = 2
ONES_ROWS = 16
VMEM_LIMIT_BYTES = 56 * 1024 * 1024
MASK_VALUE = -1e30


def _rmsnorm(x, g):
    return x * lax.rsqrt(jnp.mean(x * x, axis=-1, keepdims=True) + NORM_EPS) * g


def _swiglu(h_bf16, wg_ref, wu_ref, wd_ref):
    g = jnp.dot(h_bf16, wg_ref[...], preferred_element_type=jnp.float32)
    u = jnp.dot(h_bf16, wu_ref[...], preferred_element_type=jnp.float32)
    a = (g * jax.nn.sigmoid(g) * u).astype(jnp.bfloat16)
    return jnp.dot(a, wd_ref[...], preferred_element_type=jnp.float32)


def _rope(x, cos, sin_lo, sin_hi):
    return x * cos + pltpu.roll(x, 96, 1) * sin_lo + pltpu.roll(x, 32, 1) * sin_hi


def _ffn1_proj_kernel(x_ref, n1_ref, wg_ref, wu_ref, wd_ref, nm_ref, win_ref, invf_ref,
                      x1_ref, q_ref, k_ref, vt_ref, z_ref, cosr_scr, sinr_scr, *, tiles_per_seq, pos_offset):
    tr = x_ref.shape[0]

    @pl.when(pl.program_id(0) == 0)
    def _():
        r = lax.broadcasted_iota(jnp.int32, (tr, LANES), 0).astype(jnp.float32)
        cosr_scr[...] = jnp.cos(r * invf_ref[...])
        sinr_scr[...] = jnp.sin(r * invf_ref[...])

    x = x_ref[...]
    h = _rmsnorm(x, n1_ref[...]).astype(jnp.bfloat16)
    x1 = x + 0.5 * _swiglu(h, wg_ref, wu_ref, wd_ref)
    x1_ref[...] = x1

    hm = _rmsnorm(x1, nm_ref[...]).astype(jnp.bfloat16)
    proj = jnp.dot(hm, win_ref[...], preferred_element_type=jnp.float32)

    t0 =((pl.program_id(0) % tiles_per_seq) * tr + pos_offset).astype(jnp.float32)
    cos0 = jnp.cos(t0 * invf_ref[...])
    sin0 = jnp.sin(t0 * invf_ref[...])
    cos = cos0 * cosr_scr[...] - sin0 * sinr_scr[...]
    sin = sin0 * cosr_scr[...] + cos0 * sinr_scr[...]
    lane = lax.broadcasted_iota(jnp.int32, (tr, LANES), 1)
    first_half = (lane % HEAD_DIM) < (HEAD_DIM // 2)
    sin_lo = jnp.where(first_half, -sin, 0.0)
    sin_hi = jnp.where(first_half, 0.0, sin)
    scale = HEAD_DIM ** -0.5 * math.log2(math.e)
    for c in range(D_ATTN // LANES):
        sl = slice(c * LANES, (c + 1) * LANES)
        qc = proj[:, c * LANES:(c + 1) * LANES]
        kc = proj[:, D_ATTN + c * LANES:D_ATTN + (c + 1) * LANES]
        q_ref[:, sl] = (_rope(qc, cos, sin_lo, sin_hi) * scale).astype(jnp.bfloat16)
        k_ref[:, sl] = _rope(kc, cos, sin_lo, sin_hi).astype(jnp.bfloat16)
    hw = 2 * HEAD_DIM
    for hd in range(N_DIFF_HEADS):
        vh = proj[:, 2 * D_ATTN + hd * hw:2 * D_ATTN + (hd + 1) * hw]
        vt_ref[0, hd, 0] = vh.T.astype(jnp.bfloat16)
    ua = proj[:, 3 * D_ATTN:3 * D_ATTN + D_CONV]
    ug = proj[:, 3 * D_ATTN + D_CONV:]
    z_ref[...] = ua * jax.nn.sigmoid(ug)


def _resident(shape):
    return pl.BlockSpec(shape, lambda i: (0,) * len(shape), pipeline_mode=pl.Buffered(1))


def _ffn1_proj(x2d, n1, wg, wu, wd, nm, win, invf, *, row_tile, rows_per_seq, pos_offset):
    rows = x2d.shape[0]
    assert rows % row_tile == 0 and rows_per_seq % row_tile == 0
    tiles_per_seq = rows_per_seq // row_tile
    hw = 2 * HEAD_DIM
    row_spec = lambda w: pl.BlockSpec((row_tile, w), lambda i: (i, 0))
    vt_spec = pl.BlockSpec((1, N_DIFF_HEADS, 1, hw, row_tile),
                           lambda i: (i // tiles_per_seq, 0, i % tiles_per_seq, 0, 0))
    kern = functools.partial(_ffn1_proj_kernel, tiles_per_seq=tiles_per_seq, pos_offset=pos_offset)
    return pl.pallas_call(
        kern,
        grid=(rows // row_tile,),
        in_specs=[row_spec(D_MODEL), _resident((1, D_MODEL)),
                  _resident((D_MODEL, D_FF)), _resident((D_MODEL, D_FF)), _resident((D_FF, D_MODEL)),
                  _resident((1, D_MODEL)), _resident((D_MODEL, D_IN_PROJ)), _resident((1, LANES))],
        out_specs=[row_spec(D_MODEL), row_spec(D_ATTN), row_spec(D_ATTN), vt_spec, row_spec(D_CONV)],
        out_shape=[jax.ShapeDtypeStruct((rows, D_MODEL), jnp.float32),
                   jax.ShapeDtypeStruct((rows, D_ATTN), jnp.bfloat16),
                   jax.ShapeDtypeStruct((rows, D_ATTN), jnp.bfloat16),
                   jax.ShapeDtypeStruct((rows // rows_per_seq, N_DIFF_HEADS, tiles_per_seq, hw, row_tile),
                                        jnp.bfloat16),
                   jax.ShapeDtypeStruct((rows, D_CONV), jnp.float32)],
        scratch_shapes=[pltpu.VMEM((row_tile, LANES), jnp.float32),
                        pltpu.VMEM((row_tile, LANES), jnp.float32)],
        compiler_params=pltpu.CompilerParams(dimension_semantics=("arbitrary",),
                                             vmem_limit_bytes=VMEM_LIMIT_BYTES),
        name="ffn1_proj",
    )(x2d, n1, wg, wu, wd, nm, win, invf)


def _attn_kernel(q_ref, qn_ref, k_ref, vt_ref, km_ref, vmt_ref, lq1_ref, lk1_ref, lq2_ref, lk2_ref, sw_ref,
                 o_ref, m_scr, acc_scr, s0_scr, s1_scr, p0_scr, p1_scr, a0_scr, a1_scr, c0_scr, c1_scr):
    tq = q_ref.shape[1]
    hw, tk = vt_ref.shape[-2:]
    qi = pl.program_id(2)
    nt = (((1,), (1,)), ((), ()))
    n_chunks = 2 * tq // Q_CHUNK

    def stack_maps(q):
        lane = lax.broadcasted_iota(jnp.int32, q.shape, 1)
        zero = jnp.zeros_like(q)
        return jnp.concatenate([jnp.where(lane < HEAD_DIM, q, zero), jnp.where(lane >= HEAD_DIM, q, zero)], axis=0)

    qs = stack_maps(q_ref[0])

    def with_ones(vt):
        return jnp.concatenate([vt, jnp.ones((ONES_ROWS, vt.shape[1]), vt.dtype)], axis=0)

    s_bufs, p_bufs, a_bufs = ((s0_scr, c0_scr), (s1_scr, c1_scr)), (p0_scr, p1_scr), (a0_scr, a1_scr)

    all_lanes = [(0, 2 * tq)]

    def late_lanes(d):
        return all_lanes if d <= 0 else [(d * tk, tq), (tq + d * tk, 2 * tq)]

    def scores_into(bufs, j, q_stacked, lanes=all_lanes):
        s_scr, c_scr = bufs
        kb = k_ref[0, pl.ds(pl.multiple_of(j * tk, tk), tk), :]
        for lo, hi in lanes:
            s = lax.dot_general(kb, q_stacked[lo:hi], nt, preferred_element_type=jnp.float32)
            s_scr[:, lo:hi] = s
            c_scr[:, lo:hi] = jnp.max(s, axis=0, keepdims=True)

    def scores(j, par, lanes=all_lanes):
        scores_into(s_bufs[par], j, qs, lanes)

    def softmax(par, key_off=None, src=None):
        (s_scr, c_scr), p_scr, a_scr = (s_bufs[par] if src is None else src), p_bufs[par], a_bufs[par]
        for c in range(n_chunks):
            sl = slice(c * Q_CHUNK, (c + 1) * Q_CHUNK)
            rel = tk if key_off is None else (c * Q_CHUNK) % tq - key_off
            if rel <= -Q_CHUNK:
                continue
            if rel >= tk:
                nk = tk
                load = lambda: s_scr[:, sl]
                col_max = c_scr[:, sl]
            else:
                nk = min(tk, rel + Q_CHUNK)
                if nk < tk:
                    p_scr[nk:tk, sl] = jnp.zeros((tk - nk, Q_CHUNK), jnp.bfloat16)

                def load(nk=nk, rel=rel):
                    s = s_scr[0:nk, sl]
                    key = lax.broadcasted_iota(jnp.int32, s.shape, 0)
                    qry = lax.broadcasted_iota(jnp.int32, s.shape, 1) + rel
                    return jnp.where(key <= qry, s, MASK_VALUE)
                col_max = jnp.max(load(), axis=0, keepdims=True)
            m_prev = m_scr[:, sl]
            m_new = jnp.maximum(m_prev, col_max)
            a_scr[:, sl] = jnp.exp2(m_prev - m_new)
            p_scr[0:nk, sl] = jnp.exp2(load() - m_new).astype(jnp.bfloat16)
            m_scr[:, sl] = m_new

    def values(j, par, lanes=all_lanes):
        vtb = with_ones(vt_ref[0, 0, j])
        for lo, hi in lanes:
            acc_scr[:, lo:hi] = (a_bufs[par][:, lo:hi] * acc_scr[:, lo:hi]
                                 + jnp.dot(vtb, p_bufs[par][:, lo:hi], preferred_element_type=jnp.float32))

    def reset_max():
        m_scr[...] = jnp.full(m_scr.shape, MASK_VALUE, jnp.float32)

    def reset_acc():
        acc_scr[...] = jnp.zeros(acc_scr.shape, jnp.float32)

    def meta_softmax():
        s = lax.dot_general(km_ref[0:N_META, :], qs, nt, preferred_element_type=jnp.float32)
        m_prev = m_scr[...]
        m_new = jnp.maximum(m_prev, jnp.max(s, axis=0, keepdims=True))
        pad = jnp.zeros((vmt_ref.shape[-1] - N_META, 2 * tq), jnp.bfloat16)
        p_pad = jnp.concatenate([jnp.exp2(s - m_new).astype(jnp.bfloat16), pad], axis=0)
        return jnp.exp2(m_prev - m_new), p_pad

    def meta_values(alpha, p_pad):
        acc_scr[...] = alpha * acc_scr[...] + jnp.dot(with_ones(vmt_ref[0, 0, 0]), p_pad,
                                                      preferred_element_type=jnp.float32)

    n_diag = tq // tk
    first_diag = n_diag * qi

    def diag_step(d):
        par = d % 2
        if d + 1 < n_diag:
            scores(first_diag + d + 1, 1 - par, late_lanes(d + 1))
        softmax(par, key_off=d * tk)
        values(first_diag + d - 1, 1 - par, late_lanes(d - 1))

    def first_stages():
        scores(0, 0)
        scores(1, 1, late_lanes(1))
        reset_max()
        reset_acc()
        softmax(0, key_off=0)

    def pair(p, carry):
        t = 2 * p + 1
        scores(t + 1, 0)
        softmax(1)
        values(t - 1, 0)
        scores(t + 2, 1)
        softmax(0)
        values(t, 1)
        return carry

    lax.fori_loop(0, jnp.maximum(first_diag // 2 - 1, 0), pair, 0)

    def last_stages(is_first, with_next):
        if is_first:
            first_stages()
        else:
            scores(first_diag, 0)
            softmax(1)
            values(first_diag - 2, 0)
            scores(first_diag + 1, 1, late_lanes(1))
            softmax(0, key_off=0)
            values(first_diag - 1, 1)
        for d in range(1, n_diag - 1):
            diag_step(d)
        if with_next:
            qs_next = stack_maps(qn_ref[0])
            scores_into(s_bufs[0], 0, qs_next)
        diag_step(n_diag - 1)
        meta_alpha, meta_p = meta_softmax()
        values(first_diag + n_diag - 1, 1, late_lanes(n_diag - 1))
        meta_values(meta_alpha, meta_p)

        lam = (jnp.exp(jnp.sum(lq1_ref[...] * lk1_ref[...], axis=-1, keepdims=True))
               - jnp.exp(jnp.sum(lq2_ref[...] * lk2_ref[...], axis=-1, keepdims=True)) + LAMBDA_INIT)
        o = acc_scr[0:hw, :] / acc_scr[hw:hw + 1, :]
        o = o[:, :tq] - lam * o[:, tq:]
        o = o * lax.rsqrt(jnp.mean(o * o, axis=0, keepdims=True) + NORM_EPS)
        o_ref[0] = (o.T * sw_ref[...] * (1.0 - LAMBDA_INIT)).astype(o_ref.dtype)

        if with_next:
            reset_max()
            reset_acc()
            scores_into(s_bufs[1], 1, qs_next)
            softmax(0)

    is_last = qi == pl.num_programs(2) - 1

    @pl.when(qi == 0)
    def _():
        last_stages(is_first=True, with_next=True)

    @pl.when(jnp.logical_and(qi > 0, jnp.logical_not(is_last)))
    def _():
        last_stages(is_first=False, with_next=True)

    @pl.when(is_last)
    def _():
        last_stages(is_first=False, with_next=False)


def _diff_attention(q, k, vt, km, vmt, lq1, lk1, lq2, lk2, sw):
    b, t, _ = q.shape
    n_kv, hw, tk = vt.shape[2:]
    tq = KV_PER_Q * tk
    assert KV_PER_Q % 2 == 0 and t == n_kv * tk and t % tq == 0 and tk % Q_CHUNK == 0
    small = lambda shape: pl.BlockSpec(shape, lambda bi, hi, qi: (0,) * len(shape))
    n_q = t // tq
    assert n_q >= 2
    return pl.pallas_call(
        _attn_kernel,
        grid=(b, N_DIFF_HEADS, n_q),
        in_specs=[pl.BlockSpec((1, tq, hw), lambda bi, hi, qi: (bi, qi, hi)),
                  pl.BlockSpec((1, tq, hw), lambda bi, hi, qi: (bi, jnp.minimum(qi + 1, n_q - 1), hi)),
                  pl.BlockSpec((1, t, hw), lambda bi, hi, qi: (bi, 0, hi)),
                  pl.BlockSpec((1, 1, n_kv, hw, tk), lambda bi, hi, qi: (bi, hi, 0, 0, 0)),
                  pl.BlockSpec((km.shape[0], hw), lambda bi, hi, qi: (0, hi)),
                  pl.BlockSpec((1, 1, 1) + vmt.shape[3:], lambda bi, hi, qi: (0, hi, 0, 0, 0)),
                  small((1, HEAD_DIM)), small((1, HEAD_DIM)), small((1, HEAD_DIM)), small((1, HEAD_DIM)),
                  small((1, hw))],
        out_specs=pl.BlockSpec((1, tq, hw), lambda bi, hi, qi: (bi, qi, hi)),
        out_shape=jax.ShapeDtypeStruct((b, t, D_ATTN), jnp.bfloat16),
        scratch_shapes=[pltpu.VMEM((1, 2 * tq), jnp.float32),
                        pltpu.VMEM((hw + ONES_ROWS, 2 * tq), jnp.float32),
                        pltpu.VMEM((tk, 2 * tq), jnp.float32),
                        pltpu.VMEM((tk, 2 * tq), jnp.float32),
                        pltpu.VMEM((tk, 2 * tq), jnp.bfloat16),
                        pltpu.VMEM((tk, 2 * tq), jnp.bfloat16),
                        pltpu.VMEM((1, 2 * tq), jnp.float32),
                        pltpu.VMEM((1, 2 * tq), jnp.float32),
                        pltpu.VMEM((1, 2 * tq), jnp.float32),
                        pltpu.VMEM((1, 2 * tq), jnp.float32)],
        compiler_params=pltpu.CompilerParams(dimension_semantics=("arbitrary", "arbitrary", "arbitrary"),
                                             vmem_limit_bytes=VMEM_LIMIT_BYTES),
        name="diff_attn",
    )(q, q, k, vt, km, vmt, lq1, lk1, lq2, lk2, sw)


def _mix_ffn2_kernel(x10_ref, a0_ref, x1n_ref, an_ref, z0_ref, znext_ref, zprev_ref, zstart_ref, zero_ref,
                     cw_ref, cb_ref, lg_ref, lb_ref, wo_ref, n2_ref, wg_ref, wu_ref, wd_ref, nf_ref,
                     y_ref, zw_scr, c_scr, x2_scr, h_scr, *, tiles_per_seq):
    tr = y_ref.shape[0]
    i = pl.program_id(0)

    def fill_window(z_tile_ref, halo):
        zw_scr[0:HALO, :] = halo
        zw_scr[HALO:HALO + tr, :] = z_tile_ref[...]
        zw_scr[HALO + tr:, :] = jnp.zeros((SUBLANES, D_CONV), jnp.float32)

    def conv_rows(r0):
        base = HALO - (CONV_WIDTH - 1)
        blocks = []
        for l0 in range(0, D_CONV, LANES):
            ln = slice(l0, l0 + LANES)
            conv = None
            for rho in range(SUBLANES):
                group = None
                for o in range(rho, HALO + 1, SUBLANES):
                    if o < base:
                        continue
                    lo = r0 + o - rho
                    term = cw_ref[o - base:o - base + 1, ln] * zw_scr[lo:lo + CONV_ROWS + SUBLANES, ln]
                    group = term if group is None else group + term
                shifted = group[rho:rho + CONV_ROWS]
                conv = shifted if conv is None else conv + shifted
            blocks.append(conv)
        conv = jnp.concatenate(blocks, axis=1) + cb_ref[...]
        mu = jnp.mean(conv, axis=-1, keepdims=True)
        cc = conv - mu
        var = jnp.mean(cc * cc, axis=-1, keepdims=True)
        c = cc * lax.rsqrt(var + NORM_EPS) * lg_ref[...] + lb_ref[...]
        c = c * jax.nn.sigmoid(c)
        c_scr[r0:r0 + CONV_ROWS, :] = c.astype(jnp.bfloat16)
        bits = pltpu.bitcast(c[0:SUBLANES, 0:FF_CHUNK], jnp.int32) & zero_ref[...]
        return pltpu.bitcast(bits, jnp.float32)

    conv_starts = list(range(0, tr, CONV_ROWS))

    def mix_residual(x1_tile_ref, a_tile_ref):
        x2 = (x1_tile_ref[...]
              + jnp.dot(a_tile_ref[...], wo_ref[0:D_ATTN, :], preferred_element_type=jnp.float32)
              + jnp.dot(c_scr[...], wo_ref[D_ATTN:, :], preferred_element_type=jnp.float32))
        x2_scr[...] = x2
        h_scr[...] = _rmsnorm(x2, n2_ref[...]).astype(jnp.bfloat16)

    @pl.when(i == 0)
    def _():
        fill_window(z0_ref, zstart_ref[...])
        for r0 in conv_starts:
            conv_rows(r0)
        mix_residual(x10_ref, a0_ref)

    h = h_scr[...]

    next_starts_seq = ((i + 1) % tiles_per_seq) == 0
    fill_window(znext_ref, jnp.where(next_starts_seq, zstart_ref[...], zprev_ref[...]))
    ffn = None
    tokens = {}
    for ci, c0 in enumerate(range(0, D_FF, FF_CHUNK)):
        cols = slice(c0, c0 + FF_CHUNK)
        g = jnp.dot(h, wg_ref[:, cols], preferred_element_type=jnp.float32)
        u = jnp.dot(h, wu_ref[:, cols], preferred_element_type=jnp.float32)
        if ci in tokens:
            g = jnp.concatenate([g[0:SUBLANES] + tokens[ci], g[SUBLANES:]], axis=0)
        act = (g * jax.nn.sigmoid(g) * u).astype(jnp.bfloat16)
        d = jnp.dot(act, wd_ref[cols, :], preferred_element_type=jnp.float32)
        ffn = d if ffn is None else ffn + d
        if ci < len(conv_starts):
            tokens[ci + TOKEN_LAG] = conv_rows(conv_starts[ci])
    x3 = x2_scr[...] + 0.5 * ffn
    y_ref[...] = _rmsnorm(x3, nf_ref[...])
    mix_residual(x1n_ref, an_ref)


def _mix_ffn2(x1, a, z, zstart, cw, cb, lg, lb, wo, n2, wg, wu, wd, nf, *, rows_per_seq):
    rows = x1.shape[0]
    tr = ROW_TILE
    assert rows % tr == 0 and rows_per_seq % tr == 0 and tr % HALO == 0
    assert D_FF % FF_CHUNK == 0 and tr % CONV_ROWS == 0 and D_FF // FF_CHUNK >= tr // CONV_ROWS + TOKEN_LAG
    n_tiles = rows // tr
    row_spec = lambda w: pl.BlockSpec((tr, w), lambda i: (i, 0))
    first_spec = lambda w: pl.BlockSpec((tr, w), lambda i: (0, 0), pipeline_mode=pl.Buffered(1))
    next_spec = lambda w: pl.BlockSpec((tr, w), lambda i: (jnp.minimum(i + 1, n_tiles - 1), 0))
    z0_spec, znext_spec = first_spec(D_CONV), next_spec(D_CONV)
    halo_spec = pl.BlockSpec((HALO, D_CONV), lambda i: ((i + 1) * (tr // HALO) - 1, 0))
    kern = functools.partial(_mix_ffn2_kernel, tiles_per_seq=rows_per_seq // tr)
    return pl.pallas_call(
        kern,
        grid=(n_tiles,),
        in_specs=[first_spec(D_MODEL), first_spec(D_ATTN), next_spec(D_MODEL), next_spec(D_ATTN),
                  z0_spec, znext_spec, halo_spec,
                  _resident((HALO, D_CONV)), _resident((SUBLANES, FF_CHUNK)),
                  _resident((CONV_WIDTH, D_CONV)), _resident((1, D_CONV)),
                  _resident((1, D_CONV)), _resident((1, D_CONV)),
                  _resident((D_ATTN + D_CONV, D_MODEL)), _resident((1, D_MODEL)),
                  _resident((D_MODEL, D_FF)), _resident((D_MODEL, D_FF)), _resident((D_FF, D_MODEL)),
                  _resident((1, D_MODEL))],
        out_specs=row_spec(D_MODEL),
        out_shape=jax.ShapeDtypeStruct((rows, D_MODEL), jnp.float32),
        scratch_shapes=[pltpu.VMEM((HALO + tr + SUBLANES, D_CONV), jnp.float32),
                        pltpu.VMEM((tr, D_CONV), jnp.bfloat16),
                        pltpu.VMEM((tr, D_MODEL), jnp.float32),
                        pltpu.VMEM((tr, D_MODEL), jnp.bfloat16)],
        compiler_params=pltpu.CompilerParams(dimension_semantics=("arbitrary",),
                                             vmem_limit_bytes=VMEM_LIMIT_BYTES),
        name="mix_ffn2",
    )(x1, a, x1, a, z, z, z, zstart, jnp.zeros((SUBLANES, FF_CHUNK), jnp.int32), cw, cb, lg, lb, wo, n2, wg, wu, wd, nf)


def kernel(x, meta_tokens, ffn1_norm, ffn1_w_gate, ffn1_w_up, ffn1_w_down, mix_norm, w_in, lambda_q1, lambda_k1, lambda_q2, lambda_k2, subln_w, conv_w, conv_b, conv_ln_g, conv_ln_b, w_out, ffn2_norm, ffn2_w_gate, ffn2_w_up, ffn2_w_down, final_norm):
    b, t, d = x.shape
    bf16 = jnp.bfloat16
    row = lambda v: v.reshape(1, -1)

    inv_freq = ROPE_THETA ** (-jnp.arange(0, HEAD_DIM, 2, dtype=jnp.float32) / HEAD_DIM)
    invf = jnp.tile(inv_freq, LANES // (HEAD_DIM // 2)).reshape(1, LANES)

    ffn1_args = (row(ffn1_norm[0]), ffn1_w_gate[0].astype(bf16), ffn1_w_up[0].astype(bf16),
                 ffn1_w_down[0].astype(bf16), row(mix_norm[0]), w_in[0].astype(bf16), invf)
    x1, q, k, vt, z = _ffn1_proj(x.reshape(b * t, d), *ffn1_args,
                                 row_tile=ROW_TILE, rows_per_seq=t, pos_offset=N_META)
    meta = jnp.concatenate([meta_tokens, jnp.zeros((META_TILE - N_META, d), meta_tokens.dtype)], axis=0)
    _, _, km, vmt, zm = _ffn1_proj(meta, *ffn1_args, row_tile=META_TILE, rows_per_seq=META_TILE, pos_offset=0)

    a = _diff_attention(q.reshape(b, t, D_ATTN), k.reshape(b, t, D_ATTN), vt, km, vmt,
                        row(lambda_q1[0]), row(lambda_k1[0]), row(lambda_q2[0]), row(lambda_k2[0]),
                        row(subln_w[0]))

    zstart = jnp.concatenate([jnp.zeros((HALO - N_META, D_CONV), jnp.float32), zm[:N_META]], axis=0)
    y = _mix_ffn2(x1, a.reshape(b * t, D_ATTN), z, zstart, conv_w[0], row(conv_b[0]),
                  row(conv_ln_g[0]), row(conv_ln_b[0]), w_out[0].astype(bf16), row(ffn2_norm[0]),
                  ffn2_w_gate[0].astype(bf16), ffn2_w_up[0].astype(bf16), ffn2_w_down[0].astype(bf16),
                  row(final_norm), rows_per_seq=t)
    return y.reshape(b, t, d)
```

```python
import functools
import math

import jax
import jax.numpy as jnp
from jax import lax
from jax.experimental import pallas as pl
from jax.experimental.pallas import tpu as pltpu

D_MODEL = 1024
N_META = 16
D_ATTN = 512
D_CONV = 512
HEAD_DIM = 64
N_DIFF_HEADS = 4
CONV_WIDTH = 31
D_FF = 2816
ROPE_THETA = 10000.0
NORM_EPS = 1e-5
D_IN_PROJ = 3 * D_ATTN + 2 * D_CONV
LAMBDA_INIT = 0.8 - 0.6 * math.exp(-0.3 * 0)

LANES = 128
SUBLANES = 8
HALO = 32
ROW_TILE = 512
META_TILE = 128
KV_PER_Q = 4
Q_CHUNK = 256
FF_CHUNK = 256
CONV_ROWS = 64
TOKEN_LAG = 2
ONES_ROWS = 16
VMEM_LIMIT_BYTES = 56 * 1024 * 1024
MASK_VALUE = -1e30


def _rmsnorm(x, g):
    return x * lax.rsqrt(jnp.mean(x * x, axis=-1, keepdims=True) + NORM_EPS) * g


def _swiglu(h_bf16, wg_ref, wu_ref, wd_ref):
    g = jnp.dot(h_bf16, wg_ref[...], preferred_element_type=jnp.float32)
    u = jnp.dot(h_bf16, wu_ref[...], preferred_element_type=jnp.float32)
    a = (g * jax.nn.sigmoid(g) * u).astype(jnp.bfloat16)
    return jnp.dot(a, wd_ref[...], preferred_element_type=jnp.float32)


def _rope(x, cos, sin_lo, sin_hi):
    return x * cos + pltpu.roll(x, 96, 1) * sin_lo + pltpu.roll(x, 32, 1) * sin_hi


def _ffn1_proj_kernel(x_ref, n1_ref, wg_ref, wu_ref, wd_ref, nm_ref, win_ref, invf_ref,
                      x1_ref, q_ref, k_ref, vt_ref, z_ref, cosr_scr, sinr_scr, *, tiles_per_seq, pos_offset):
    tr = x_ref.shape[0]

    @pl.when(pl.program_id(0) == 0)
    def _():
        r = lax.broadcasted_iota(jnp.int32, (tr, LANES), 0).astype(jnp.float32)
        cosr_scr[...] = jnp.cos(r * invf_ref[...])
        sinr_scr[...] = jnp.sin(r * invf_ref[...])

    x = x_ref[...]
    h = _rmsnorm(x, n1_ref[...]).astype(jnp.bfloat16)
    x1 = x + _swiglu(h, wg_ref, wu_ref, wd_ref)
    x1_ref[...] = x1

    hm = _rmsnorm(x1, nm_ref[...]).astype(jnp.bfloat16)
    proj = jnp.dot(hm, win_ref[...], preferred_element_type=jnp.float32)

    t0 =((pl.program_id(0) % tiles_per_seq) * tr + pos_offset).astype(jnp.float32)
    cos0 = jnp.cos(t0 * invf_ref[...])
    sin0 = jnp.sin(t0 * invf_ref[...])
    cos = cos0 * cosr_scr[...] - sin0 * sinr_scr[...]
    sin = sin0 * cosr_scr[...] + cos0 * sinr_scr[...]
    lane = lax.broadcasted_iota(jnp.int32, (tr, LANES), 1)
    first_half = (lane % HEAD_DIM) < (HEAD_DIM // 2)
    sin_lo = jnp.where(first_half, -sin, 0.0)
    sin_hi = jnp.where(first_half, 0.0, sin)
    scale = HEAD_DIM ** -0.5 * math.log2(math.e)
    for c in range(D_ATTN // LANES):
        sl = slice(c * LANES, (c + 1) * LANES)
        qc = proj[:, c * LANES:(c + 1) * LANES]
        kc = proj[:, D_ATTN + c * LANES:D_ATTN + (c + 1) * LANES]
        q_ref[:, sl] = (_rope(qc, cos, sin_lo, sin_hi) * scale).astype(jnp.bfloat16)
        k_ref[:, sl] = _rope(kc, cos, sin_lo, sin_hi).astype(jnp.bfloat16)
    hw = 2 * HEAD_DIM
    for hd in range(N_DIFF_HEADS):
        vh = proj[:, 2 * D_ATTN + hd * hw:2 * D_ATTN + (hd + 1) * hw]
        vt_ref[0, hd, 0] = vh.T.astype(jnp.bfloat16)
    ua = proj[:, 3 * D_ATTN:3 * D_ATTN + D_CONV]
    ug = proj[:, 3 * D_ATTN + D_CONV:]
    z_ref[...] = ua * jax.nn.sigmoid(ug)


def _resident(shape):
    return pl.BlockSpec(shape, lambda i: (0,) * len(shape), pipeline_mode=pl.Buffered(1))


def _ffn1_proj(x2d, n1, wg, wu, wd, nm, win, invf, *, row_tile, rows_per_seq, pos_offset):
    rows = x2d.shape[0]
    assert rows % row_tile == 0 and rows_per_seq % row_tile == 0
    tiles_per_seq = rows_per_seq // row_tile
    hw = 2 * HEAD_DIM
    row_spec = lambda w: pl.BlockSpec((row_tile, w), lambda i: (i, 0))
    vt_spec = pl.BlockSpec((1, N_DIFF_HEADS, 1, hw, row_tile),
                           lambda i: (i // tiles_per_seq, 0, i % tiles_per_seq, 0, 0))
    kern = functools.partial(_ffn1_proj_kernel, tiles_per_seq=tiles_per_seq, pos_offset=pos_offset)
    return pl.pallas_call(
        kern,
        grid=(rows // row_tile,),
        in_specs=[row_spec(D_MODEL), _resident((1, D_MODEL)),
                  _resident((D_MODEL, D_FF)), _resident((D_MODEL, D_FF)), _resident((D_FF, D_MODEL)),
                  _resident((1, D_MODEL)), _resident((D_MODEL, D_IN_PROJ)), _resident((1, LANES))],
        out_specs=[row_spec(D_MODEL), row_spec(D_ATTN), row_spec(D_ATTN), vt_spec, row_spec(D_CONV)],
        out_shape=[jax.ShapeDtypeStruct((rows, D_MODEL), jnp.float32),
                   jax.ShapeDtypeStruct((rows, D_ATTN), jnp.bfloat16),
                   jax.ShapeDtypeStruct((rows, D_ATTN), jnp.bfloat16),
                   jax.ShapeDtypeStruct((rows // rows_per_seq, N_DIFF_HEADS, tiles_per_seq, hw, row_tile),
                                        jnp.bfloat16),
                   jax.ShapeDtypeStruct((rows, D_CONV), jnp.float32)],
        scratch_shapes=[pltpu.VMEM((row_tile, LANES), jnp.float32),
                        pltpu.VMEM((row_tile, LANES), jnp.float32)],
        compiler_params=pltpu.CompilerParams(dimension_semantics=("arbitrary",),
                                             vmem_limit_bytes=VMEM_LIMIT_BYTES),
        name="ffn1_proj",
    )(x2d, n1, wg, wu, wd, nm, win, invf)


def _attn_kernel(q_ref, qn_ref, k_ref, vt_ref, km_ref, vmt_ref, lq1_ref, lk1_ref, lq2_ref, lk2_ref, sw_ref,
                 o_ref, m_scr, acc_scr, s0_scr, s1_scr, p0_scr, p1_scr, a0_scr, a1_scr, c0_scr, c1_scr):
    tq = q_ref.shape[1]
    hw, tk = vt_ref.shape[-2:]
    qi = pl.program_id(2)
    nt = (((1,), (1,)), ((), ()))
    n_chunks = 2 * tq // Q_CHUNK

    def stack_maps(q):
        lane = lax.broadcasted_iota(jnp.int32, q.shape, 1)
        zero = jnp.zeros_like(q)
        return jnp.concatenate([jnp.where(lane < HEAD_DIM, q, zero), jnp.where(lane >= HEAD_DIM, q, zero)], axis=0)

    qs = stack_maps(q_ref[0])

    def with_ones(vt):
        return jnp.concatenate([vt, jnp.ones((ONES_ROWS, vt.shape[1]), vt.dtype)], axis=0)

    s_bufs, p_bufs, a_bufs = ((s0_scr, c0_scr), (s1_scr, c1_scr)), (p0_scr, p1_scr), (a0_scr, a1_scr)

    all_lanes = [(0, 2 * tq)]

    def late_lanes(d):
        return all_lanes if d <= 0 else [(d * tk, tq), (tq + d * tk, 2 * tq)]

    def scores_into(bufs, j, q_stacked, lanes=all_lanes):
        s_scr, c_scr = bufs
        kb = k_ref[0, pl.ds(pl.multiple_of(j * tk, tk), tk), :]
        for lo, hi in lanes:
            s = lax.dot_general(kb, q_stacked[lo:hi], nt, preferred_element_type=jnp.float32)
            s_scr[:, lo:hi] = s
            c_scr[:, lo:hi] = jnp.max(s, axis=0, keepdims=True)

    def scores(j, par, lanes=all_lanes):
        scores_into(s_bufs[par], j, qs, lanes)

    def softmax(par, key_off=None, src=None):
        (s_scr, c_scr), p_scr, a_scr = (s_bufs[par] if src is None else src), p_bufs[par], a_bufs[par]
        for c in range(n_chunks):
            sl = slice(c * Q_CHUNK, (c + 1) * Q_CHUNK)
            rel = tk if key_off is None else (c * Q_CHUNK) % tq - key_off
            if rel <= -Q_CHUNK:
                continue
            if rel >= tk:
                nk = tk
                load = lambda: s_scr[:, sl]
                col_max = c_scr[:, sl]
            else:
                nk = min(tk, rel + Q_CHUNK)
                if nk < tk:
                    p_scr[nk:tk, sl] = jnp.zeros((tk - nk, Q_CHUNK), jnp.bfloat16)

                def load(nk=nk, rel=rel):
                    s = s_scr[0:nk, sl]
                    key = lax.broadcasted_iota(jnp.int32, s.shape, 0)
                    qry = lax.broadcasted_iota(jnp.int32, s.shape, 1) + rel
                    return jnp.where(key <= qry, s, MASK_VALUE)
                col_max = jnp.max(load(), axis=0, keepdims=True)
            m_prev = m_scr[:, sl]
            m_new = jnp.maximum(m_prev, col_max)
            a_scr[:, sl] = jnp.exp2(m_prev - m_new)
            p_scr[0:nk, sl] = jnp.exp2(load() - m_new).astype(jnp.bfloat16)
            m_scr[:, sl] = m_new

    def values(j, par, lanes=all_lanes):
        vtb = with_ones(vt_ref[0, 0, j])
        for lo, hi in lanes:
            acc_scr[:, lo:hi] = (a_bufs[par][:, lo:hi] * acc_scr[:, lo:hi]
                                 + jnp.dot(vtb, p_bufs[par][:, lo:hi], preferred_element_type=jnp.float32))

    def reset_max():
        m_scr[...] = jnp.full(m_scr.shape, MASK_VALUE, jnp.float32)

    def reset_acc():
        acc_scr[...] = jnp.zeros(acc_scr.shape, jnp.float32)

    def meta_softmax():
        s = lax.dot_general(km_ref[0:N_META, :], qs, nt, preferred_element_type=jnp.float32)
        m_prev = m_scr[...]
        m_new = jnp.maximum(m_prev, jnp.max(s, axis=0, keepdims=True))
        pad = jnp.zeros((vmt_ref.shape[-1] - N_META, 2 * tq), jnp.bfloat16)
        p_pad = jnp.concatenate([jnp.exp2(s - m_new).astype(jnp.bfloat16), pad], axis=0)
        return jnp.exp2(m_prev - m_new), p_pad

    def meta_values(alpha, p_pad):
        acc_scr[...] = alpha * acc_scr[...] + jnp.dot(with_ones(vmt_ref[0, 0, 0]), p_pad,
                                                      preferred_element_type=jnp.float32)

    n_diag = tq // tk
    first_diag = n_diag * qi

    def diag_step(d):
        par = d % 2
        if d + 1 < n_diag:
            scores(first_diag + d + 1, 1 - par, late_lanes(d + 1))
        softmax(par, key_off=d * tk)
        values(first_diag + d - 1, 1 - par, late_lanes(d - 1))

    @pl.when(qi == 0)
    def _():
        scores(0, 0)
        scores(1, 1, late_lanes(1))
        reset_max()
        reset_acc()
        softmax(0, key_off=0)

    def pair(p, carry):
        t = 2 * p + 1
        scores(t + 1, 0)
        softmax(1)
        values(t - 1, 0)
        scores(t + 2, 1)
        softmax(0)
        values(t, 1)
        return carry

    lax.fori_loop(0, jnp.maximum(first_diag // 2 - 1, 0), pair, 0)

    @pl.when(qi > 0)
    def _():
        scores(first_diag, 0)
        softmax(1)
        values(first_diag - 2, 0)
        scores(first_diag + 1, 1, late_lanes(1))
        softmax(0, key_off=0)
        values(first_diag - 1, 1)

    def last_stages(with_next):
        for d in range(1, n_diag - 1):
            diag_step(d)
        if with_next:
            qs_next = stack_maps(qn_ref[0])
            scores_into(s_bufs[0], 0, qs_next)
        diag_step(n_diag - 1)
        meta_alpha, meta_p = meta_softmax()
        values(first_diag + n_diag - 1, 1, late_lanes(n_diag - 1))
        meta_values(meta_alpha, meta_p)

        lam = (jnp.exp(jnp.sum(lq1_ref[...] * lk1_ref[...], axis=-1, keepdims=True))
               - jnp.exp(jnp.sum(lq2_ref[...] * lk2_ref[...], axis=-1, keepdims=True)) + LAMBDA_INIT)
        o = acc_scr[0:hw, :] / acc_scr[hw:hw + 1, :]
        o = o[:, :tq] - lam * o[:, tq:]
        o = o * lax.rsqrt(jnp.mean(o * o, axis=0, keepdims=True) + NORM_EPS)
        o_ref[0] = (o.T * sw_ref[...] * (1.0 - LAMBDA_INIT)).astype(o_ref.dtype)

        if with_next:
            reset_max()
            reset_acc()
            scores_into(s_bufs[1], 1, qs_next)
            softmax(0)

    is_last = qi == pl.num_programs(2) - 1

    @pl.when(jnp.logical_not(is_last))
    def _():
        last_stages(with_next=True)

    @pl.when(is_last)
    def _():
        last_stages(with_next=False)


def _diff_attention(q, k, vt, km, vmt, lq1, lk1, lq2, lk2, sw):
    b, t, _ = q.shape
    n_kv, hw, tk = vt.shape[2:]
    tq = KV_PER_Q * tk
    assert KV_PER_Q % 2 == 0 and t == n_kv * tk and t % tq == 0 and tk % Q_CHUNK == 0
    small = lambda shape: pl.BlockSpec(shape, lambda bi, hi, qi: (0,) * len(shape))
    n_q = t // tq
    return pl.pallas_call(
        _attn_kernel,
        grid=(b, N_DIFF_HEADS, n_q),
        in_specs=[pl.BlockSpec((1, tq, hw), lambda bi, hi, qi: (bi, qi, hi)),
                  pl.BlockSpec((1, tq, hw), lambda bi, hi, qi: (bi, jnp.minimum(qi + 1, n_q - 1), hi)),
                  pl.BlockSpec((1, t, hw), lambda bi, hi, qi: (bi, 0, hi)),
                  pl.BlockSpec((1, 1, n_kv, hw, tk), lambda bi, hi, qi: (bi, hi, 0, 0, 0)),
                  pl.BlockSpec((km.shape[0], hw), lambda bi, hi, qi: (0, hi)),
                  pl.BlockSpec((1, 1, 1) + vmt.shape[3:], lambda bi, hi, qi: (0, hi, 0, 0, 0)),
                  small((1, HEAD_DIM)), small((1, HEAD_DIM)), small((1, HEAD_DIM)), small((1, HEAD_DIM)),
                  small((1, hw))],
        out_specs=pl.BlockSpec((1, tq, hw), lambda bi, hi, qi: (bi, qi, hi)),
        out_shape=jax.ShapeDtypeStruct((b, t, D_ATTN), jnp.bfloat16),
        scratch_shapes=[pltpu.VMEM((1, 2 * tq), jnp.float32),
                        pltpu.VMEM((hw + ONES_ROWS, 2 * tq), jnp.float32),
                        pltpu.VMEM((tk, 2 * tq), jnp.float32),
                        pltpu.VMEM((tk, 2 * tq), jnp.float32),
                        pltpu.VMEM((tk, 2 * tq), jnp.bfloat16),
                        pltpu.VMEM((tk, 2 * tq), jnp.bfloat16),
                        pltpu.VMEM((1, 2 * tq), jnp.float32),
                        pltpu.VMEM((1, 2 * tq), jnp.float32),
                        pltpu.VMEM((1, 2 * tq), jnp.float32),
                        pltpu.VMEM((1, 2 * tq), jnp.float32)],
        compiler_params=pltpu.CompilerParams(dimension_semantics=("arbitrary", "arbitrary", "arbitrary"),
                                             vmem_limit_bytes=VMEM_LIMIT_BYTES),
        name="diff_attn",
    )(q, q, k, vt, km, vmt, lq1, lk1, lq2, lk2, sw)


def _mix_ffn2_kernel(x10_ref, a0_ref, x1n_ref, an_ref, z0_ref, znext_ref, zprev_ref, zstart_ref, zero_ref,
                     cw_ref, cb_ref, lg_ref, lb_ref, wo_ref, n2_ref, wg_ref, wu_ref, wd_ref, nf_ref,
                     y_ref, zw_scr, c_scr, x2_scr, h_scr, *, tiles_per_seq):
    tr = y_ref.shape[0]
    i = pl.program_id(0)

    def fill_window(z_tile_ref, halo):
        zw_scr[0:HALO, :] = halo
        zw_scr[HALO:HALO + tr, :] = z_tile_ref[...]
        zw_scr[HALO + tr:, :] = jnp.zeros((SUBLANES, D_CONV), jnp.float32)

    def conv_rows(r0):
        base = HALO - (CONV_WIDTH - 1)
        blocks = []
        for l0 in range(0, D_CONV, LANES):
            ln = slice(l0, l0 + LANES)
            conv = None
            for rho in range(SUBLANES):
                group = None
                for o in range(rho, HALO + 1, SUBLANES):
                    if o < base:
                        continue
                    lo = r0 + o - rho
                    term = cw_ref[o - base:o - base + 1, ln] * zw_scr[lo:lo + CONV_ROWS + SUBLANES, ln]
                    group = term if group is None else group + term
                shifted = group[rho:rho + CONV_ROWS]
                conv = shifted if conv is None else conv + shifted
            blocks.append(conv)
        conv = jnp.concatenate(blocks, axis=1) + cb_ref[...]
        mu = jnp.mean(conv, axis=-1, keepdims=True)
        cc = conv - mu
        var = jnp.mean(cc * cc, axis=-1, keepdims=True)
        c = cc * lax.rsqrt(var + NORM_EPS) * lg_ref[...] + lb_ref[...]
        c = c * jax.nn.sigmoid(c)
        c_scr[r0:r0 + CONV_ROWS, :] = c.astype(jnp.bfloat16)
        bits = pltpu.bitcast(c[0:SUBLANES, 0:FF_CHUNK], jnp.int32) & zero_ref[...]
        return pltpu.bitcast(bits, jnp.float32)

    conv_starts = list(range(0, tr, CONV_ROWS))

    def mix_residual(x1_tile_ref, a_tile_ref):
        x2 = (x1_tile_ref[...]
              + jnp.dot(a_tile_ref[...], wo_ref[0:D_ATTN, :], preferred_element_type=jnp.float32)
              + jnp.dot(c_scr[...], wo_ref[D_ATTN:, :], preferred_element_type=jnp.float32))
        x2_scr[...] = x2
        h_scr[...] = _rmsnorm(x2, n2_ref[...]).astype(jnp.bfloat16)

    @pl.when(i == 0)
    def _():
        fill_window(z0_ref, zstart_ref[...])
        for r0 in conv_starts:
            conv_rows(r0)
        mix_residual(x10_ref, a0_ref)

    h = h_scr[...]

    next_starts_seq = ((i + 1) % tiles_per_seq) == 0
    fill_window(znext_ref, jnp.where(next_starts_seq, zstart_ref[...], zprev_ref[...]))
    ffn = None
    tokens = {}
    for ci, c0 in enumerate(range(0, D_FF, FF_CHUNK)):
        cols = slice(c0, c0 + FF_CHUNK)
        g = jnp.dot(h, wg_ref[:, cols], preferred_element_type=jnp.float32)
        u = jnp.dot(h, wu_ref[:, cols], preferred_element_type=jnp.float32)
        if ci in tokens:
            g = jnp.concatenate([g[0:SUBLANES] + tokens[ci], g[SUBLANES:]], axis=0)
        act = (g * jax.nn.sigmoid(g) * u).astype(jnp.bfloat16)
        d = jnp.dot(act, wd_ref[cols, :], preferred_element_type=jnp.float32)
        ffn = d if ffn is None else ffn + d
        if ci < len(conv_starts):
            tokens[ci + TOKEN_LAG] = conv_rows(conv_starts[ci])
    x3 = x2_scr[...] + ffn
    y_ref[...] = _rmsnorm(x3, nf_ref[...])
    mix_residual(x1n_ref, an_ref)


def _mix_ffn2(x1, a, z, zstart, cw, cb, lg, lb, wo, n2, wg, wu, wd, nf, *, rows_per_seq):
    rows = x1.shape[0]
    tr = ROW_TILE
    assert rows % tr == 0 and rows_per_seq % tr == 0 and tr % HALO == 0
    assert D_FF % FF_CHUNK == 0 and tr % CONV_ROWS == 0 and D_FF // FF_CHUNK >= tr // CONV_ROWS + TOKEN_LAG
    n_tiles = rows // tr
    row_spec = lambda w: pl.BlockSpec((tr, w), lambda i: (i, 0))
    first_spec = lambda w: pl.BlockSpec((tr, w), lambda i: (0, 0), pipeline_mode=pl.Buffered(1))
    next_spec = lambda w: pl.BlockSpec((tr, w), lambda i: (jnp.minimum(i + 1, n_tiles - 1), 0))
    z0_spec, znext_spec = first_spec(D_CONV), next_spec(D_CONV)
    halo_spec = pl.BlockSpec((HALO, D_CONV), lambda i: ((i + 1) * (tr // HALO) - 1, 0))
    kern = functools.partial(_mix_ffn2_kernel, tiles_per_seq=rows_per_seq // tr)
    return pl.pallas_call(
        kern,
        grid=(n_tiles,),
        in_specs=[first_spec(D_MODEL), first_spec(D_ATTN), next_spec(D_MODEL), next_spec(D_ATTN),
                  z0_spec, znext_spec, halo_spec,
                  _resident((HALO, D_CONV)), _resident((SUBLANES, FF_CHUNK)),
                  _resident((CONV_WIDTH, D_CONV)), _resident((1, D_CONV)),
                  _resident((1, D_CONV)), _resident((1, D_CONV)),
                  _resident((D_ATTN + D_CONV, D_MODEL)), _resident((1, D_MODEL)),
                  _resident((D_MODEL, D_FF)), _resident((D_MODEL, D_FF)), _resident((D_FF, D_MODEL)),
                  _resident((1, D_MODEL))],
        out_specs=row_spec(D_MODEL),
        out_shape=jax.ShapeDtypeStruct((rows, D_MODEL), jnp.float32),
        scratch_shapes=[pltpu.VMEM((HALO + tr + SUBLANES, D_CONV), jnp.float32),
                        pltpu.VMEM((tr, D_CONV), jnp.bfloat16),
                        pltpu.VMEM((tr, D_MODEL), jnp.float32),
                        pltpu.VMEM((tr, D_MODEL), jnp.bfloat16)],
        compiler_params=pltpu.CompilerParams(dimension_semantics=("arbitrary",),
                                             vmem_limit_bytes=VMEM_LIMIT_BYTES),
        name="mix_ffn2",
    )(x1, a, x1, a, z, z, z, zstart, jnp.zeros((SUBLANES, FF_CHUNK), jnp.int32), cw, cb, lg, lb, wo, n2, wg, wu, wd, nf)


def kernel(x, meta_tokens, ffn1_norm, ffn1_w_gate, ffn1_w_up, ffn1_w_down, mix_norm, w_in, lambda_q1, lambda_k1, lambda_q2, lambda_k2, subln_w, conv_w, conv_b, conv_ln_g, conv_ln_b, w_out, ffn2_norm, ffn2_w_gate, ffn2_w_up, ffn2_w_down, final_norm):
    b, t, d = x.shape
    bf16 = jnp.bfloat16
    row = lambda v: v.reshape(1, -1)

    inv_freq = ROPE_THETA ** (-jnp.arange(0, HEAD_DIM, 2, dtype=jnp.float32) / HEAD_DIM)
    invf = jnp.tile(inv_freq, LANES // (HEAD_DIM // 2)).reshape(1, LANES)

    ffn1_args = (row(ffn1_norm[0]), ffn1_w_gate[0].astype(bf16), ffn1_w_up[0].astype(bf16),
                 (0.5 * ffn1_w_down[0]).astype(bf16), row(mix_norm[0]), w_in[0].astype(bf16), invf)
    x1, q, k, vt, z = _ffn1_proj(x.reshape(b * t, d), *ffn1_args,
                                 row_tile=ROW_TILE, rows_per_seq=t, pos_offset=N_META)
    meta = jnp.concatenate([meta_tokens, jnp.zeros((META_TILE - N_META, d), meta_tokens.dtype)], axis=0)
    _, _, km, vmt, zm = _ffn1_proj(meta, *ffn1_args, row_tile=META_TILE, rows_per_seq=META_TILE, pos_offset=0)

    a = _diff_attention(q.reshape(b, t, D_ATTN), k.reshape(b, t, D_ATTN), vt, km, vmt,
                        row(lambda_q1[0]), row(lambda_k1[0]), row(lambda_q2[0]), row(lambda_k2[0]),
                        row(subln_w[0]))

    zstart = jnp.concatenate([jnp.zeros((HALO - N_META, D_CONV), jnp.float32), zm[:N_META]], axis=0)
    y = _mix_ffn2(x1, a.reshape(b * t, D_ATTN), z, zstart, conv_w[0], row(conv_b[0]),
                  row(conv_ln_g[0]), row(conv_ln_b[0]), w_out[0].astype(bf16), row(ffn2_norm[0]),
                  ffn2_w_gate[0].astype(bf16), ffn2_w_up[0].astype(bf16), (0.5 * ffn2_w_down[0]).astype(bf16),
                  row(final_norm), rows_per_seq=t)
    return y.reshape(b, t, d)
```
